```python
import jax, jax.numpy as jnp
from jax import lax
import numpy as np

D_MODEL = 1024
BATCH = 16
SEQ = 256
DEPTH = 4
DEC_BATCH = 4
DEC_SEQ = 2048
PAST_LEN = 256

GRID_W = 64
CONV_K = 31
MLA_HEADS = 16
QK_NOPE = 64
QK_ROPE = 32
V_HEAD = 64
Q_LORA = 512
KV_LORA = 256
ROPE_THETA = 10000.0
PEER_HEADS = 8
PEER_QDIM = 256
N_KEYS = 128
N_EXPERTS = N_KEYS * N_KEYS
PEER_TOPK = 16
Q_BLOCK = 128
TOKEN_BLOCK = 128
EPS = 1e-6
N_CONV_LAYERS = (DEPTH + 1) // 2
N_ATTN_LAYERS = DEPTH // 2

kernel_name = "hybrid_conv_mla_peer_diffusion_step"


def rmsnorm(x, g):
    xf = x.astype(jnp.float32)
    y = xf * lax.rsqrt(jnp.mean(xf * xf, axis=-1, keepdims=True) + EPS)
    return (y * g.astype(jnp.float32)).astype(x.dtype)


def modulation(cond, w, b):
    m = jax.nn.silu(cond) @ w + b
    return jnp.split(m[:, None, :], 6, axis=-1)


def modulate(x, g, shift, scale):
    return rmsnorm(x, g) * (1 + scale) + shift


def conv_module(x, w_in, b_in, dw, dw_b, norm_g, w_out, b_out):
    h = x @ w_in + b_in
    a, gt = jnp.split(h, 2, axis=-1)
    h = a * jax.nn.sigmoid(gt)
    h = lax.conv_general_dilated(h, dw[:, None, :], window_strides=(1,),
                                 padding=[(CONV_K // 2, CONV_K // 2)],
                                 dimension_numbers=('NWC', 'WIO', 'NWC'),
                                 feature_group_count=D_MODEL) + dw_b
    h = jax.nn.silu(rmsnorm(h, norm_g))
    return h @ w_out + b_out


def axial_rope_tables(seq_len):
    rows = seq_len // GRID_W
    row = jnp.repeat(jnp.arange(rows), GRID_W).astype(jnp.float32)
    col = jnp.tile(jnp.arange(GRID_W), rows).astype(jnp.float32)
    half = QK_ROPE // 2
    inv = ROPE_THETA ** (-(jnp.arange(half // 2, dtype=jnp.float32) * 2.0 / half))
    ang = jnp.concatenate([row[:, None] * inv, col[:, None] * inv], axis=-1)
    return jnp.cos(ang), jnp.sin(ang)


def apply_rope(x, cos, sin):
    xf = x.astype(jnp.float32).reshape(x.shape[:-1] + (-1, 2))
    x1, x2 = xf[..., 0], xf[..., 1]
    out = jnp.stack([x1 * cos - x2 * sin, x1 * sin + x2 * cos], axis=-1).reshape(x.shape)
    return out.astype(x.dtype)


def rope_tail(x, cos, sin):
    return jnp.concatenate([x[..., :QK_NOPE], apply_rope(x[..., QK_NOPE:], cos, sin)], axis=-1)


def mla_queries(x, q_a_w, q_a_norm, q_b_w, q_norm):
    B, S, _ = x.shape
    q = (rmsnorm(x @ q_a_w, q_a_norm) @ q_b_w).reshape(B, S, MLA_HEADS, QK_NOPE + QK_ROPE)
    return rmsnorm(q, q_norm)


def mla_latent(x, kv_a_w, kv_a_norm):
    kv = x @ kv_a_w
    return rmsnorm(kv[..., :KV_LORA], kv_a_norm), kv[..., KV_LORA:]


def mla_keys_values(ckv, kpe, kv_b_w, k_norm):
    B, S, _ = ckv.shape
    kv = (ckv @ kv_b_w).reshape(B, S, MLA_HEADS, QK_NOPE + V_HEAD)
    k_nope, v = kv[..., :QK_NOPE], kv[..., QK_NOPE:]
    k_pe = jnp.broadcast_to(kpe[:, :, None, :], (B, S, MLA_HEADS, QK_ROPE))
    k = rmsnorm(jnp.concatenate([k_nope, k_pe], axis=-1), k_norm)
    return k, v


def blocked_attention(q, k, v):
    B, Sq, H, Dk = q.shape
    nblk = Sq // Q_BLOCK
    qb = q.reshape(B, nblk, Q_BLOCK, H, Dk).transpose(1, 0, 2, 3, 4)
    scale = Dk ** -0.5

    def one(qblk):
        s = jnp.einsum('bqhd,bkhd->bhqk', qblk, k).astype(jnp.float32) * scale
        p = jax.nn.softmax(s, axis=-1).astype(v.dtype)
        return jnp.einsum('bhqk,bkhd->bqhd', p, v)

    o = lax.map(one, qb)
    return o.transpose(1, 0, 2, 3, 4).reshape(B, Sq, H * v.shape[-1])


def peer_ffn(x, wq, keys, u, v):
    B, S, D = x.shape
    xt = x.reshape(-1, TOKEN_BLOCK, D)

    def one(xb):
        T = xb.shape[0]
        q = (xb @ wq).reshape(T, PEER_HEADS, 2, PEER_QDIM // 2)
        s = jnp.einsum('thpd,pnd->thpn', q, keys).astype(jnp.float32)
        sv, si = lax.top_k(s, PEER_TOPK)
        cand = sv[:, :, 0, :, None] + sv[:, :, 1, None, :]
        cidx = si[:, :, 0, :, None] * N_KEYS + si[:, :, 1, None, :]
        top_s, top_pos = lax.top_k(cand.reshape(T, PEER_HEADS, PEER_TOPK * PEER_TOPK), PEER_TOPK)
        eidx = jnp.take_along_axis(cidx.reshape(T, PEER_HEADS, PEER_TOPK * PEER_TOPK), top_pos, axis=-1)
        g = jax.nn.softmax(top_s, axis=-1)
        ue = u[eidx]
        ve = v[eidx]
        act = jax.nn.gelu(jnp.einsum('thkd,td->thk', ue, xb).astype(jnp.float32), approximate=False)
        w = (g * act).astype(xb.dtype)
        return jnp.einsum('thk,thkd->td', w, ve)

    return lax.map(one, xt).reshape(B, S, D)


def setup_inputs(seed: int = 0) -> dict:
    key = jax.random.key(seed)
    ks = iter(jax.random.split(key, 40))

    def nrm(shape, scale):
        return jax.random.normal(next(ks), shape, jnp.float32) * scale

    def gain(shape):
        return 1.0 + nrm(shape, 0.02)

    D = D_MODEL
    NC, NA = N_CONV_LAYERS, N_ATTN_LAYERS
    QK = QK_NOPE + QK_ROPE
    return {
        "x_prompt": nrm((BATCH, SEQ, D), 1.0),
        "x_sample": nrm((DEC_BATCH, DEC_SEQ, D), 1.0),
        "cache_ckv": nrm((DEC_BATCH, NA, PAST_LEN, KV_LORA), 1.0),
        "cache_kpe": nrm((DEC_BATCH, NA, PAST_LEN, QK_ROPE), 1.0),
        "c": nrm((DEC_BATCH, D), 1.0),
        "c_ctx": nrm((D,), 1.0),
        "mod_w": nrm((DEPTH, D, 6 * D), 0.5 * D ** -0.5),
        "mod_b": nrm((DEPTH, 6 * D), 0.02),
        "norm1_g": gain((DEPTH, D)),
        "norm2_g": gain((DEPTH, D)),
        "conv_w_in": nrm((NC, D, 2 * D), D ** -0.5),
        "conv_b_in": nrm((NC, 2 * D), 0.02),
        "conv_dw": nrm((NC, CONV_K, D), CONV_K ** -0.5),
        "conv_dw_b": nrm((NC, D), 0.02),
        "conv_norm_g": gain((NC, D)),
        "conv_w_out": nrm((NC, D, D), D ** -0.5),
        "conv_b_out": nrm((NC, D), 0.02),
        "q_a_w": nrm((NA, D, Q_LORA), D ** -0.5),
        "q_a_norm": gain((NA, Q_LORA)),
        "q_b_w": nrm((NA, Q_LORA, MLA_HEADS * QK), Q_LORA ** -0.5),
        "kv_a_w": nrm((NA, D, KV_LORA + QK_ROPE), D ** -0.5),
        "kv_a_norm": gain((NA, KV_LORA)),
        "kv_b_w": nrm((NA, KV_LORA, MLA_HEADS * (QK_NOPE + V_HEAD)), KV_LORA ** -0.5),
        "q_norm": gain((NA, QK)),
        "k_norm": gain((NA, QK)),
        "o_w": nrm((NA, MLA_HEADS * V_HEAD, D), (MLA_HEADS * V_HEAD) ** -0.5),
        "peer_wq": nrm((DEPTH, D, PEER_HEADS * PEER_QDIM), D ** -0.5),
        "peer_keys": nrm((DEPTH, 2, N_KEYS, PEER_QDIM // 2), (PEER_QDIM // 2) ** -0.5),
        "peer_u": nrm((DEPTH, N_EXPERTS, D), D ** -0.5),
        "peer_v": nrm((DEPTH, N_EXPERTS, D), PEER_HEADS ** -0.5),
    }


def reference(x_prompt, x_sample, cache_ckv, cache_kpe, c, c_ctx, mod_w, mod_b, norm1_g, norm2_g,
              conv_w_in, conv_b_in, conv_dw, conv_dw_b, conv_norm_g, conv_w_out, conv_b_out,
              q_a_w, q_a_norm, q_b_w, kv_a_w, kv_a_norm, kv_b_w, q_norm, k_norm, o_w,
              peer_wq, peer_keys, peer_u, peer_v):
    n_prompt = x_prompt.shape[0]
    cond_p = jnp.broadcast_to(c_ctx[None, :], (n_prompt, D_MODEL))
    cos, sin = axial_rope_tables(x_sample.shape[1])
    cos_h, sin_h = cos[:, None, :], sin[:, None, :]
    h_p, h_s = x_prompt, x_sample
    new_ckv, new_kpe = [], []
    for i in range(DEPTH):
        sh1_p, sc1_p, g1_p, sh2_p, sc2_p, g2_p = modulation(cond_p, mod_w[i], mod_b[i])
        sh1_s, sc1_s, g1_s, sh2_s, sc2_s, g2_s = modulation(c, mod_w[i], mod_b[i])
        n_p = modulate(h_p, norm1_g[i], sh1_p, sc1_p)
        n_s = modulate(h_s, norm1_g[i], sh1_s, sc1_s)
        j = i // 2
        if i % 2 == 0:
            out_p = conv_module(n_p, conv_w_in[j], conv_b_in[j], conv_dw[j], conv_dw_b[j],
                                conv_norm_g[j], conv_w_out[j], conv_b_out[j])
            out_s = conv_module(n_s, conv_w_in[j], conv_b_in[j], conv_dw[j], conv_dw_b[j],
                                conv_norm_g[j], conv_w_out[j], conv_b_out[j])
        else:
            ckv_p, kpe_p = mla_latent(n_p, kv_a_w[j], kv_a_norm[j])
            new_ckv.append(ckv_p)
            new_kpe.append(kpe_p)
            k_p, v_p = mla_keys_values(ckv_p, kpe_p, kv_b_w[j], k_norm[j])
            q_p = mla_queries(n_p, q_a_w[j], q_a_norm[j], q_b_w[j], q_norm[j])
            out_p = blocked_attention(q_p, k_p, v_p) @ o_w[j]
            k_c, v_c = mla_keys_values(cache_ckv[:, j], cache_kpe[:, j], kv_b_w[j], k_norm[j])
            ckv_s, kpe_s = mla_latent(n_s, kv_a_w[j], kv_a_norm[j])
            k_s, v_s = mla_keys_values(ckv_s, kpe_s, kv_b_w[j], k_norm[j])
            q_s = mla_queries(n_s, q_a_w[j], q_a_norm[j], q_b_w[j], q_norm[j])
            q_s = rope_tail(q_s, cos_h, sin_h)
            k_s = rope_tail(k_s, cos_h, sin_h)
            out_s = blocked_attention(q_s, jnp.concatenate([k_c, k_s], axis=1),
                                      jnp.concatenate([v_c, v_s], axis=1)) @ o_w[j]
        h_p = h_p + g1_p * out_p
        h_s = h_s + g1_s * out_s
        n_p = modulate(h_p, norm2_g[i], sh2_p, sc2_p)
        n_s = modulate(h_s, norm2_g[i], sh2_s, sc2_s)
        h_p = h_p + g2_p * peer_ffn(n_p, peer_wq[i], peer_keys[i], peer_u[i], peer_v[i])
        h_s = h_s + g2_s * peer_ffn(n_s, peer_wq[i], peer_keys[i], peer_u[i], peer_v[i])
    new_ckv_arr = jnp.stack(new_ckv, axis=1)
    new_kpe_arr = jnp.stack(new_kpe, axis=1)
    return (h_p, h_s, new_ckv_arr, new_kpe_arr)
```

```python
import functools

import jax
import jax.numpy as jnp
from jax import lax
from jax.experimental import pallas as pl
from jax.experimental.pallas import tpu as pltpu

F32 = jnp.float32
BF16 = jnp.bfloat16

D_MODEL = 1024
GRID_W = 64
CONV_K = 31
MLA_HEADS = 16
QK_NOPE = 64
QK_ROPE = 32
V_HEAD = 64
QK_DIM = QK_NOPE + QK_ROPE
ROPE_THETA = 10000.0
PEER_HEADS = 8
N_KEYS = 128
PEER_TOPK = 16
EPS = 1e-6

LANES = 128
TM = 256
HALO = 16
TB = 8
ROWS_PER_TOKEN = PEER_HEADS * PEER_TOPK
VMEM_LIMIT = 48 * 1024 * 1024


def _cparams(n_grid):
    return pltpu.CompilerParams(
        dimension_semantics=("arbitrary",) * n_grid, vmem_limit_bytes=VMEM_LIMIT)


def _rms(x, g, denom=None):
    n = x.shape[-1] if denom is None else denom
    ms = jnp.sum(x * x, axis=-1, keepdims=True) / float(n)
    return x * lax.rsqrt(ms + EPS) * g


def _modulate(h, g, shift, scale):
    return _rms(h, g) * (1.0 + scale) + shift


def _gelu(x):
    return 0.5 * x * (1.0 + lax.erf(x * (0.5 ** 0.5)))


def _mod_slice(m, k):
    return m[:, k * D_MODEL:(k + 1) * D_MODEL]


def _mod_kernel(c_ref, w_ref, b_ref, o_ref):
    c = c_ref[...]
    a = (c * jax.nn.sigmoid(c)).astype(BF16)
    o_ref[...] = jnp.dot(a, w_ref[...].astype(BF16), preferred_element_type=F32) + b_ref[...]


def _modulation_all(cond8, mod_w, mod_b):
    depth, d, n6 = mod_w.shape
    tn = 1536
    return pl.pallas_call(
        _mod_kernel,
        grid=(depth, n6 // tn),
        in_specs=[
            pl.BlockSpec((8, d), lambda l, n: (0, 0)),
            pl.BlockSpec((None, d, tn), lambda l, n: (l, 0, n)),
            pl.BlockSpec((None, 1, tn), lambda l, n: (l, 0, n)),
        ],
        out_specs=pl.BlockSpec((None, 8, tn), lambda l, n: (l, 0, n)),
        out_shape=jax.ShapeDtypeStruct((depth, 8, n6), F32),
        compiler_params=_cparams(2),
        name="modulation",
    )(cond8, mod_w, mod_b.reshape(depth, 1, n6))


def _merge_kernel(u_ref, v_ref, o_ref):
    d = u_ref.shape[-1]
    o_ref[:, :d] = u_ref[...]
    o_ref[:, d:] = v_ref[...]


def _merge_tables(u, v):
    depth, n_exp, d = u.shape
    tr = 512
    return pl.pallas_call(
        _merge_kernel,
        grid=(depth, n_exp // tr),
        in_specs=[pl.BlockSpec((None, tr, d), lambda l, r: (l, r, 0)),
                  pl.BlockSpec((None, tr, d), lambda l, r: (l, r, 0))],
        out_specs=pl.BlockSpec((None, tr, 2 * d), lambda l, r: (l, r, 0)),
        out_shape=jax.ShapeDtypeStruct((depth, n_exp, 2 * d), F32),
        compiler_params=_cparams(2),
        name="merge_tables",
    )(u, v)


def _conv_in_kernel(cond_ref, h_ref, m_ref, g_ref, w_ref, b_ref, o_ref):
    del cond_ref
    m = m_ref[...]
    n = _modulate(h_ref[...], g_ref[...], _mod_slice(m, 0), _mod_slice(m, 1))
    hh = jnp.dot(n.astype(BF16), w_ref[...], preferred_element_type=F32) + b_ref[...]
    o_ref[...] = hh[:, :D_MODEL] * jax.nn.sigmoid(hh[:, D_MODEL:])


def _conv_out_kernel(first_ref, last_ref, cond_ref, cur_ref, prev_ref, next_ref, dw_ref, dwb_ref, ng_ref,
                     w_ref, b_ref, h_ref, m_ref, o_ref, pad_ref, conv_ref):
    del cond_ref
    i = pl.program_id(0)
    keep_prev = jnp.where(first_ref[i] == 0, 1.0, 0.0)
    keep_next = jnp.where(last_ref[i] == 0, 1.0, 0.0)
    pad_ref[0:HALO, :] = prev_ref[...] * keep_prev
    pad_ref[HALO:HALO + TM, :] = cur_ref[...]
    pad_ref[HALO + TM:HALO + TM + HALO, :] = next_ref[...] * keep_next
    base = HALO - CONV_K // 2
    for c in range(D_MODEL // LANES):
        cols = slice(c * LANES, (c + 1) * LANES)
        acc = jnp.zeros((TM, LANES), F32) + dwb_ref[:, cols]
        for k in range(CONV_K):
            acc = acc + dw_ref[k:k + 1, cols] * pad_ref[base + k:base + k + TM, cols]
        conv_ref[:, cols] = acc
    y = _rms(conv_ref[...], ng_ref[...])
    y = y * jax.nn.sigmoid(y)
    out = jnp.dot(y.astype(BF16), w_ref[...], preferred_element_type=F32) + b_ref[...]
    o_ref[...] = h_ref[...] + _mod_slice(m_ref[...], 2) * out


def _conv_layer(h, mod, cond_of_tile, seq_first, seq_last, g, w_in, b_in, dw, dw_b, ng, w_out, b_out):
    n_tok, d = h.shape
    nt = n_tok // TM
    hb = TM // HALO
    tile = lambda i, *_: (i, 0)
    full = lambda i, *_: (0, 0)
    modmap = lambda i, cond, *_: (cond[i], 0, 0)
    glu = pl.pallas_call(
        _conv_in_kernel,
        grid_spec=pltpu.PrefetchScalarGridSpec(
            num_scalar_prefetch=1, grid=(nt,),
            in_specs=[
                pl.BlockSpec((TM, d), lambda i, cond: (i, 0)),
                pl.BlockSpec((None, 1, 6 * d), lambda i, cond: (cond[i], 0, 0)),
                pl.BlockSpec((1, d), lambda i, cond: (0, 0)),
                pl.BlockSpec((d, 2 * d), lambda i, cond: (0, 0)),
                pl.BlockSpec((1, 2 * d), lambda i, cond: (0, 0)),
            ],
            out_specs=pl.BlockSpec((TM, d), lambda i, cond: (i, 0))),
        out_shape=jax.ShapeDtypeStruct((n_tok, d), F32),
        compiler_params=_cparams(1),
        name="conv_in",
    )(cond_of_tile, h, mod, g.reshape(1, d), w_in, b_in.reshape(1, 2 * d))

    n_hb = n_tok // HALO
    dw_pad = jnp.zeros((32, d), F32).at[:CONV_K].set(dw)
    return pl.pallas_call(
        _conv_out_kernel,
        grid_spec=pltpu.PrefetchScalarGridSpec(
            num_scalar_prefetch=3, grid=(nt,),
            in_specs=[
                pl.BlockSpec((TM, d), lambda i, f, l, cond: (i, 0)),
                pl.BlockSpec((HALO, d), lambda i, f, l, cond: (jnp.maximum(i * hb - 1, 0), 0)),
                pl.BlockSpec((HALO, d), lambda i, f, l, cond: (jnp.minimum((i + 1) * hb, n_hb - 1), 0)),
                pl.BlockSpec((32, d), lambda i, f, l, cond: (0, 0)),
                pl.BlockSpec((1, d), lambda i, f, l, cond: (0, 0)),
                pl.BlockSpec((1, d), lambda i, f, l, cond: (0, 0)),
                pl.BlockSpec((d, d), lambda i, f, l, cond: (0, 0)),
                pl.BlockSpec((1, d), lambda i, f, l, cond: (0, 0)),
                pl.BlockSpec((TM, d), lambda i, f, l, cond: (i, 0)),
                pl.BlockSpec((None, 1, 6 * d), lambda i, f, l, cond: (cond[i], 0, 0)),
            ],
            out_specs=pl.BlockSpec((TM, d), lambda i, f, l, cond: (i, 0)),
            scratch_shapes=[pltpu.VMEM((TM + 2 * HALO, d), F32), pltpu.VMEM((TM, d), F32)]),
        out_shape=jax.ShapeDtypeStruct((n_tok, d), F32),
        compiler_params=_cparams(1),
        name="conv_out",
    )(seq_first, seq_last, cond_of_tile, glu, glu, glu, dw_pad, dw_b.reshape(1, d), ng.reshape(1, d),
      w_out, b_out.reshape(1, d), h, mod)


def _rope(y, cos, sin):
    lane = lax.broadcasted_iota(jnp.int32, y.shape, 1)
    nxt = pltpu.roll(y, LANES - 1, 1)
    prv = pltpu.roll(y, 1, 1)
    partner = jnp.where((lane & 1) == 0, nxt, prv)
    return y * cos + partner * sin


def _attn_latent_kernel(cond_ref, h_ref, m_ref, g_ref, qaw_ref, qan_ref, kvw_ref, kvn_ref, kpew_ref,
                        qlat_ref, ckv_ref, kpe_ref):
    del cond_ref
    m = m_ref[...]
    n = _modulate(h_ref[...], g_ref[...], _mod_slice(m, 0), _mod_slice(m, 1)).astype(BF16)
    qa = jnp.dot(n, qaw_ref[...], preferred_element_type=F32)
    qlat_ref[...] = _rms(qa, qan_ref[...]).astype(BF16)
    kv = jnp.dot(n, kvw_ref[...], preferred_element_type=F32)
    ckv_ref[...] = _rms(kv, kvn_ref[...])
    kpe_ref[...] = jnp.dot(n, kpew_ref[...], preferred_element_type=F32)


def _q_kernel(qlat_ref, w_ref, qn_ref, cos_ref, sin_ref, o_ref):
    q = jnp.dot(qlat_ref[...], w_ref[...], preferred_element_type=F32)
    cos = cos_ref[...]
    sin = sin_ref[...]
    qn = qn_ref[...]
    for hd in range(MLA_HEADS):
        cols = slice(hd * LANES, (hd + 1) * LANES)
        y = _rms(q[:, cols], qn, denom=QK_DIM)
        o_ref[:, cols] = _rope(y, cos, sin).astype(BF16)


def _kv_kernel(ckv_ref, kpe_ref, wk_ref, wv_ref, kn_ref, cos_ref, sin_ref, k_ref, v_ref):
    c = ckv_ref[...].astype(BF16)
    kf = jnp.dot(c, wk_ref[...], preferred_element_type=F32)
    v_ref[...] = jnp.dot(c, wv_ref[...], preferred_element_type=F32).astype(BF16)
    kpe = kpe_ref[...]
    cos = cos_ref[...]
    sin = sin_ref[...]
    kn = kn_ref[...]
    for hd in range(MLA_HEADS):
        cols = slice(hd * LANES, (hd + 1) * LANES)
        y = _rms(kf[:, cols] + kpe, kn, denom=QK_DIM)
        k_ref[:, cols] = _rope(y, cos, sin).astype(BF16)


def _attn_kernel(q_ref, k_ref, v_ref, o_ref):
    scale = float(QK_DIM) ** -0.5
    for hh in range(2):
        q = q_ref[:, hh * LANES:(hh + 1) * LANES]
        k = k_ref[:, hh * LANES:(hh + 1) * LANES]
        s = lax.dot_general(q, k, (((1,), (1,)), ((), ())), preferred_element_type=F32) * scale
        p = jnp.exp(s - jnp.max(s, axis=-1, keepdims=True))
        l = jnp.sum(p, axis=-1, keepdims=True)
        o = jnp.dot(p.astype(BF16), v_ref[:, hh * V_HEAD:(hh + 1) * V_HEAD], preferred_element_type=F32)
        o_ref[:, hh * V_HEAD:(hh + 1) * V_HEAD] = (o / l).astype(BF16)


def _attn_out_kernel(cond_ref, op_ref, os_ref, w_ref, h_ref, m_ref, o_ref, *, n_prompt_tiles):
    del cond_ref
    i = pl.program_id(0)
    o = jnp.where(i < n_prompt_tiles, op_ref[...], os_ref[...])
    out = jnp.dot(o, w_ref[...], preferred_element_type=F32)
    o_ref[...] = h_ref[...] + _mod_slice(m_ref[...], 2) * out


def _attention(q, k, v, n_batch, sq, sk, q_tile_offset):
    nq = sq // TM
    hp = MLA_HEADS // 2
    return pl.pallas_call(
        _attn_kernel,
        grid=(n_batch, hp, nq),
        in_specs=[
            pl.BlockSpec((TM, 2 * LANES), lambda b, h, qi: (q_tile_offset + b * nq + qi, h)),
            pl.BlockSpec((sk, 2 * LANES), lambda b, h, qi: (b, h)),
            pl.BlockSpec((sk, 2 * V_HEAD), lambda b, h, qi: (b, h)),
        ],
        out_specs=pl.BlockSpec((TM, 2 * V_HEAD), lambda b, h, qi: (b * nq + qi, h)),
        out_shape=jax.ShapeDtypeStruct((n_batch * sq, MLA_HEADS * V_HEAD), BF16),
        compiler_params=_cparams(3),
        name="attention",
    )(q, k, v)


def _kv_expand(ckv, kpe, wk, wv, kn, cos_t, sin_t, table_block):
    n_rows = ckv.shape[0]
    nt = n_rows // TM
    return pl.pallas_call(
        _kv_kernel,
        grid=(nt,),
        in_specs=[
            pl.BlockSpec((TM, ckv.shape[1]), lambda i: (i, 0)),
            pl.BlockSpec((TM, LANES), lambda i: (i, 0)),
            pl.BlockSpec(wk.shape, lambda i: (0, 0)),
            pl.BlockSpec(wv.shape, lambda i: (0, 0)),
            pl.BlockSpec((1, LANES), lambda i: (0, 0)),
            pl.BlockSpec((TM, LANES), lambda i: (table_block(i), 0)),
            pl.BlockSpec((TM, LANES), lambda i: (table_block(i), 0)),
        ],
        out_specs=[pl.BlockSpec((TM, MLA_HEADS * LANES), lambda i: (i, 0)),
                   pl.BlockSpec((TM, MLA_HEADS * V_HEAD), lambda i: (i, 0))],
        out_shape=[jax.ShapeDtypeStruct((n_rows, MLA_HEADS * LANES), BF16),
                   jax.ShapeDtypeStruct((n_rows, MLA_HEADS * V_HEAD), BF16)],
        compiler_params=_cparams(1),
        name="kv_expand",
    )(ckv, kpe, wk, wv, kn, cos_t, sin_t)


def _attn_layer(h, mod, cond_of_tile, g, w, cache_ckv, cache_kpe, cos_t, sin_t, dims):
    n_tok, d = h.shape
    nt = n_tok // TM
    n_prompt, seq, n_dec, dec_seq, past = dims
    npt = n_prompt * seq // TM
    tps = dec_seq // TM
    q_lora = w["q_a_w"].shape[1]
    kv_lora = w["kv_w"].shape[1]

    qlat, ckv, kpe = pl.pallas_call(
        _attn_latent_kernel,
        grid_spec=pltpu.PrefetchScalarGridSpec(
            num_scalar_prefetch=1, grid=(nt,),
            in_specs=[
                pl.BlockSpec((TM, d), lambda i, cond: (i, 0)),
                pl.BlockSpec((None, 1, 6 * d), lambda i, cond: (cond[i], 0, 0)),
                pl.BlockSpec((1, d), lambda i, cond: (0, 0)),
                pl.BlockSpec((d, q_lora), lambda i, cond: (0, 0)),
                pl.BlockSpec((1, q_lora), lambda i, cond: (0, 0)),
                pl.BlockSpec((d, kv_lora), lambda i, cond: (0, 0)),
                pl.BlockSpec((1, kv_lora), lambda i, cond: (0, 0)),
                pl.BlockSpec((d, LANES), lambda i, cond: (0, 0)),
            ],
            out_specs=[pl.BlockSpec((TM, q_lora), lambda i, cond: (i, 0)),
                       pl.BlockSpec((TM, kv_lora), lambda i, cond: (i, 0)),
                       pl.BlockSpec((TM, LANES), lambda i, cond: (i, 0))]),
        out_shape=[jax.ShapeDtypeStruct((n_tok, q_lora), BF16),
                   jax.ShapeDtypeStruct((n_tok, kv_lora), F32),
                   jax.ShapeDtypeStruct((n_tok, LANES), F32)],
        compiler_params=_cparams(1),
        name="attn_latent",
    )(cond_of_tile, h, mod, g.reshape(1, d), w["q_a_w"], w["q_a_norm"], w["kv_w"], w["kv_norm"], w["kpe_w"])

    q_table = lambda i: jnp.where(i < npt, 0, 1 + (i - npt) % tps)
    q = pl.pallas_call(
        _q_kernel,
        grid=(nt,),
        in_specs=[
            pl.BlockSpec((TM, q_lora), lambda i: (i, 0)),
            pl.BlockSpec(w["q_b_w"].shape, lambda i: (0, 0)),
            pl.BlockSpec((1, LANES), lambda i: (0, 0)),
            pl.BlockSpec((TM, LANES), lambda i: (q_table(i), 0)),
            pl.BlockSpec((TM, LANES), lambda i: (q_table(i), 0)),
        ],
        out_specs=pl.BlockSpec((TM, MLA_HEADS * LANES), lambda i: (i, 0)),
        out_shape=jax.ShapeDtypeStruct((n_tok, MLA_HEADS * LANES), BF16),
        compiler_params=_cparams(1),
        name="q_expand",
    )(qlat, w["q_b_w"], w["q_norm"], cos_t, sin_t)

    n_p_rows = n_prompt * seq
    ckv_p, kpe_p = ckv[:n_p_rows], kpe[:n_p_rows]
    k_p, v_p = _kv_expand(ckv_p, kpe_p, w["k_w"], w["v_w"], w["k_norm"], cos_t, sin_t, lambda i: 0)
    ckv_s = jnp.concatenate([cache_ckv, ckv[n_p_rows:].reshape(n_dec, dec_seq, kv_lora)], axis=1)
    kpe_s = jnp.concatenate([cache_kpe, kpe[n_p_rows:].reshape(n_dec, dec_seq, LANES)], axis=1)
    sk = past + dec_seq
    tpk = sk // TM
    pc = past // TM
    k_s, v_s = _kv_expand(ckv_s.reshape(n_dec * sk, kv_lora), kpe_s.reshape(n_dec * sk, LANES),
                          w["k_w"], w["v_w"], w["k_norm"], cos_t, sin_t,
                          lambda i: jnp.where(i % tpk < pc, 0, 1 + i % tpk - pc))

    o_p = _attention(q, k_p, v_p, n_prompt, seq, seq, 0)
    o_s = _attention(q, k_s, v_s, n_dec, dec_seq, sk, npt)

    h_new = pl.pallas_call(
        functools.partial(_attn_out_kernel, n_prompt_tiles=npt),
        grid_spec=pltpu.PrefetchScalarGridSpec(
            num_scalar_prefetch=1, grid=(nt,),
            in_specs=[
                pl.BlockSpec((TM, d), lambda i, cond: (jnp.minimum(i, npt - 1), 0)),
                pl.BlockSpec((TM, d), lambda i, cond: (jnp.maximum(i - npt, 0), 0)),
                pl.BlockSpec((d, d), lambda i, cond: (0, 0)),
                pl.BlockSpec((TM, d), lambda i, cond: (i, 0)),
                pl.BlockSpec((None, 1, 6 * d), lambda i, cond: (cond[i], 0, 0)),
            ],
            out_specs=pl.BlockSpec((TM, d), lambda i, cond: (i, 0))),
        out_shape=jax.ShapeDtypeStruct((n_tok, d), F32),
        compiler_params=_cparams(1),
        name="attn_out",
    )(cond_of_tile, o_p, o_s, w["o_w"], h, mod)
    return h_new, ckv_p, kpe_p[:, QK_NOPE:QK_NOPE + QK_ROPE]


def _topk_rows(s, payload, k):
    n = s.shape[0]
    row = lax.broadcasted_iota(jnp.int32, s.shape, 0).astype(F32)
    vals, pays = [], []
    for _ in range(k):
        m = jnp.max(s, axis=0, keepdims=True)
        pos = jnp.min(jnp.where(s == m, row, float(n)), axis=0, keepdims=True)
        hit = row == pos
        if payload is None:
            pays.append(pos)
        else:
            pays.append(jnp.max(jnp.where(hit, payload, -1.0), axis=0, keepdims=True))
        vals.append(m)
        s = jnp.where(hit, -jnp.inf, s)
    return jnp.concatenate(vals, axis=0), jnp.concatenate(pays, axis=0)


def _peer_route_kernel(cond_ref, h_ref, m_ref, g_ref, wqt_ref, keys_ref, n_ref, idx_ref, gate_ref,
                       qt_ref, sv_ref, si_ref, e_ref, gt_ref):
    del cond_ref
    m = m_ref[...]
    n = _modulate(h_ref[...], g_ref[...], _mod_slice(m, 3), _mod_slice(m, 4))
    n_ref[...] = n
    qt_ref[...] = lax.dot_general(wqt_ref[...], n.astype(BF16), (((1,), (1,)), ((), ())),
                                  preferred_element_type=F32)

    def sub_key(hp, carry):
        q = qt_ref[pl.ds(pl.multiple_of(hp * N_KEYS, N_KEYS), N_KEYS), :].astype(BF16)
        s = jnp.dot(keys_ref[hp % 2], q, preferred_element_type=F32)
        vals, pos = _topk_rows(s, None, PEER_TOPK)
        sv_ref[hp] = vals
        si_ref[hp] = pos
        return carry

    lax.fori_loop(0, 2 * PEER_HEADS, sub_key, 0)

    def head(hd, carry):
        sv0, sv1 = sv_ref[2 * hd], sv_ref[2 * hd + 1]
        si0, si1 = si_ref[2 * hd], si_ref[2 * hd + 1]
        cand = jnp.concatenate([sv0[a:a + 1] + sv1 for a in range(PEER_TOPK)], axis=0)
        cidx = jnp.concatenate([si0[a:a + 1] * float(N_KEYS) + si1 for a in range(PEER_TOPK)], axis=0)
        top_s, top_e = _topk_rows(cand, cidx, PEER_TOPK)
        p = jnp.exp(top_s - top_s[0:1])
        rows = pl.ds(pl.multiple_of(hd * PEER_TOPK, PEER_TOPK), PEER_TOPK)
        gt_ref[rows, :] = p / jnp.sum(p, axis=0, keepdims=True)
        e_ref[rows, :] = top_e
        return carry

    lax.fori_loop(0, PEER_HEADS, head, 0)
    idx_ref[...] = e_ref[...].T.astype(jnp.int32)
    gate_ref[...] = gt_ref[...].T


def _peer_route(h, mod, cond_of_tile, g, wqt, keys):
    n_tok, d = h.shape
    nt = n_tok // TM
    nq = wqt.shape[0]
    return pl.pallas_call(
        _peer_route_kernel,
        grid_spec=pltpu.PrefetchScalarGridSpec(
            num_scalar_prefetch=1, grid=(nt,),
            in_specs=[
                pl.BlockSpec((TM, d), lambda i, cond: (i, 0)),
                pl.BlockSpec((None, 1, 6 * d), lambda i, cond: (cond[i], 0, 0)),
                pl.BlockSpec((1, d), lambda i, cond: (0, 0)),
                pl.BlockSpec((nq, d), lambda i, cond: (0, 0)),
                pl.BlockSpec(keys.shape, lambda i, cond: (0, 0, 0)),
            ],
            out_specs=[pl.BlockSpec((TM, d), lambda i, cond: (i, 0)),
                       pl.BlockSpec((TM, ROWS_PER_TOKEN), lambda i, cond: (i, 0)),
                       pl.BlockSpec((TM, ROWS_PER_TOKEN), lambda i, cond: (i, 0))],
            scratch_shapes=[
                pltpu.VMEM((nq, TM), F32),
                pltpu.VMEM((2 * PEER_HEADS, PEER_TOPK, TM), F32),
                pltpu.VMEM((2 * PEER_HEADS, PEER_TOPK, TM), F32),
                pltpu.VMEM((ROWS_PER_TOKEN, TM), F32),
                pltpu.VMEM((ROWS_PER_TOKEN, TM), F32),
            ]),
        out_shape=[jax.ShapeDtypeStruct((n_tok, d), F32),
                   jax.ShapeDtypeStruct((n_tok, ROWS_PER_TOKEN), jnp.int32),
                   jax.ShapeDtypeStruct((n_tok, ROWS_PER_TOKEN), F32)],
        compiler_params=_cparams(1),
        name="peer_route",
    )(cond_of_tile, h, mod, g.reshape(1, d), wqt, keys)


def _peer_apply_kernel(cond_ref, idx_hbm, gate_ref, n_ref, h_ref, m_ref, tab_all_hbm, o_ref,
                       idx_smem, rows, idx_sem, row_sem, *, layer):
    del cond_ref
    tab_hbm = tab_all_hbm.at[layer]
    i = pl.program_id(0)
    n_steps = pl.num_programs(0)
    n_rows = TB * ROWS_PER_TOKEN
    d = D_MODEL

    def idx_copy(step, slot):
        return pltpu.make_async_copy(idx_hbm.at[step], idx_smem.at[slot], idx_sem.at[slot])

    def issue_rows(slot):
        def body(r, carry):
            e = idx_smem[slot, r]
            pltpu.make_async_copy(tab_hbm.at[pl.ds(e, 1)], rows.at[slot, pl.ds(r, 1)], row_sem.at[slot]).start()
            return carry
        lax.fori_loop(0, n_rows, body, 0, unroll=8)

    slot = i % 2
    nslot = 1 - slot

    @pl.when(i == 0)
    def _():
        idx_copy(0, 0).start()
        idx_copy(0, 0).wait()
        issue_rows(0)

        @pl.when(n_steps > 1)
        def _():
            idx_copy(1, 1).start()

    @pl.when(i + 1 < n_steps)
    def _():
        idx_copy(i + 1, nslot).wait()
        issue_rows(nslot)

        @pl.when(i + 2 < n_steps)
        def _():
            idx_copy(i + 2, slot).start()

    pltpu.make_async_copy(tab_hbm.at[pl.ds(0, n_rows)], rows.at[slot], row_sem.at[slot]).wait()

    x = n_ref[...]
    gcol = gate_ref[...].T
    outs = []
    for t in range(TB):
        r0 = t * ROWS_PER_TOKEN
        u = rows[slot, r0:r0 + ROWS_PER_TOKEN, 0:d]
        s = jnp.sum(u * x[t:t + 1, :], axis=-1, keepdims=True)
        wgt = gcol[:, t:t + 1] * _gelu(s)
        v = rows[slot, r0:r0 + ROWS_PER_TOKEN, d:2 * d]
        outs.append(jnp.sum(wgt * v, axis=0, keepdims=True))
    out = jnp.concatenate(outs, axis=0)
    o_ref[...] = h_ref[...] + _mod_slice(m_ref[...], 5) * out


def _peer_apply(h, n, idx, gate, mod, cond_of_step, table, layer):
    n_tok, d = h.shape
    n_steps = n_tok // TB
    idx_steps = idx.reshape(n_steps, TB * ROWS_PER_TOKEN)
    return pl.pallas_call(
        functools.partial(_peer_apply_kernel, layer=layer),
        grid_spec=pltpu.PrefetchScalarGridSpec(
            num_scalar_prefetch=1, grid=(n_steps,),
            in_specs=[
                pl.BlockSpec(memory_space=pl.ANY),
                pl.BlockSpec((TB, ROWS_PER_TOKEN), lambda i, cond: (i, 0)),
                pl.BlockSpec((TB, d), lambda i, cond: (i, 0)),
                pl.BlockSpec((TB, d), lambda i, cond: (i, 0)),
                pl.BlockSpec((None, 1, 6 * d), lambda i, cond: (cond[i], 0, 0)),
                pl.BlockSpec(memory_space=pl.ANY),
            ],
            out_specs=pl.BlockSpec((TB, d), lambda i, cond: (i, 0)),
            scratch_shapes=[
                pltpu.SMEM((2, TB * ROWS_PER_TOKEN), jnp.int32),
                pltpu.VMEM((2, TB * ROWS_PER_TOKEN, 2 * d), F32),
                pltpu.SemaphoreType.DMA((2,)),
                pltpu.SemaphoreType.DMA((2,)),
            ]),
        out_shape=jax.ShapeDtypeStruct((n_tok, d), F32),
        compiler_params=_cparams(1),
        name="peer_apply",
    )(cond_of_step, idx_steps, gate, n, h, mod, table)


def _rope_tables(seq_len):
    rows = seq_len // GRID_W
    row = jnp.repeat(jnp.arange(rows), GRID_W).astype(F32)
    col = jnp.tile(jnp.arange(GRID_W), rows).astype(F32)
    half = QK_ROPE // 2
    inv = ROPE_THETA ** (-(jnp.arange(half // 2, dtype=F32) * 2.0 / half))
    ang = jnp.concatenate([row[:, None] * inv, col[:, None] * inv], axis=-1)
    cos, sin = jnp.cos(ang), jnp.sin(ang)
    cos_pairs = jnp.repeat(cos, 2, axis=-1)
    sin_pairs = jnp.stack([-sin, sin], axis=-1).reshape(seq_len, QK_ROPE)
    cos_t = jnp.ones((TM + seq_len, LANES), F32).at[TM:, QK_NOPE:QK_DIM].set(cos_pairs)
    sin_t = jnp.zeros((TM + seq_len, LANES), F32).at[TM:, QK_NOPE:QK_DIM].set(sin_pairs)
    return cos_t, sin_t


def _pad_heads(w, width):
    k = w.shape[0]
    w3 = w.reshape(k, MLA_HEADS, width)
    return jnp.pad(w3, ((0, 0), (0, 0), (0, LANES - width))).reshape(k, MLA_HEADS * LANES)


def _attn_weights(j, q_a_w, q_a_norm, q_b_w, kv_a_w, kv_a_norm, kv_b_w, q_norm, k_norm, o_w):
    d = q_a_w.shape[1]
    kv_lora = kv_a_norm.shape[1]
    kvb = kv_b_w[j].reshape(kv_lora, MLA_HEADS, QK_NOPE + V_HEAD)
    lane_pad = (0, LANES - QK_DIM)
    return {
        "q_a_w": q_a_w[j].astype(BF16),
        "q_a_norm": q_a_norm[j].reshape(1, -1),
        "q_b_w": _pad_heads(q_b_w[j], QK_DIM).astype(BF16),
        "kv_w": kv_a_w[j][:, :kv_lora].astype(BF16),
        "kv_norm": kv_a_norm[j].reshape(1, -1),
        "kpe_w": jnp.pad(kv_a_w[j][:, kv_lora:], ((0, 0), (QK_NOPE, LANES - QK_DIM))).astype(BF16),
        "k_w": _pad_heads(kvb[:, :, :QK_NOPE].reshape(kv_lora, MLA_HEADS * QK_NOPE), QK_NOPE).astype(BF16),
        "v_w": kvb[:, :, QK_NOPE:].reshape(kv_lora, MLA_HEADS * V_HEAD).astype(BF16),
        "q_norm": jnp.pad(q_norm[j], lane_pad).reshape(1, LANES),
        "k_norm": jnp.pad(k_norm[j], lane_pad).reshape(1, LANES),
        "o_w": o_w[j].astype(BF16),
    }


def kernel(x_prompt, x_sample, cache_ckv, cache_kpe, c, c_ctx, mod_w, mod_b, norm1_g, norm2_g, conv_w_in, conv_b_in, conv_dw, conv_dw_b, conv_norm_g, conv_w_out, conv_b_out, q_a_w, q_a_norm, q_b_w, kv_a_w, kv_a_norm, kv_b_w, q_norm, k_norm, o_w, peer_wq, peer_keys, peer_u, peer_v):
    n_prompt, seq, d = x_prompt.shape
    n_dec, dec_seq, _ = x_sample.shape
    past = cache_ckv.shape[2]
    depth = mod_w.shape[0]
    assert d == D_MODEL and seq % TM == 0 and dec_seq % TM == 0 and past % TM == 0
    assert 1 + n_dec <= 8
    dims = (n_prompt, seq, n_dec, dec_seq, past)

    n_p_rows = n_prompt * seq
    h = jnp.concatenate([x_prompt.reshape(n_p_rows, d), x_sample.reshape(n_dec * dec_seq, d)], axis=0)
    n_tok = h.shape[0]

    tiles_p, tps = n_p_rows // TM, dec_seq // TM
    tile = jnp.arange(n_tok // TM, dtype=jnp.int32)
    in_p = tile < tiles_p
    cond_of_tile = jnp.where(in_p, 0, 1 + (tile - tiles_p) // tps).astype(jnp.int32)
    pos_p, pos_s = tile % (seq // TM), (tile - tiles_p) % tps
    seq_first = jnp.where(in_p, pos_p == 0, pos_s == 0).astype(jnp.int32)
    seq_last = jnp.where(in_p, pos_p == seq // TM - 1, pos_s == tps - 1).astype(jnp.int32)
    cond_of_step = jnp.repeat(cond_of_tile, TM // TB)

    cond8 = jnp.zeros((8, d), F32).at[0].set(c_ctx).at[1:1 + n_dec].set(c)
    mod_all = _modulation_all(cond8, mod_w, mod_b)
    table = _merge_tables(peer_u, peer_v)
    cos_t, sin_t = _rope_tables(dec_seq)
    cache_kpe_pad = jnp.pad(cache_kpe, ((0, 0), (0, 0), (0, 0), (QK_NOPE, LANES - QK_DIM)))

    new_ckv, new_kpe = [], []
    for i in range(depth):
        mod = mod_all[i].reshape(8, 1, 6 * d)
        j = i // 2
        if i % 2 == 0:
            h = _conv_layer(h, mod, cond_of_tile, seq_first, seq_last, norm1_g[i],
                            conv_w_in[j].astype(BF16), conv_b_in[j], conv_dw[j], conv_dw_b[j],
                            conv_norm_g[j], conv_w_out[j].astype(BF16), conv_b_out[j])
        else:
            w = _attn_weights(j, q_a_w, q_a_norm, q_b_w, kv_a_w, kv_a_norm, kv_b_w, q_norm, k_norm, o_w)
            h, ckv_p, kpe_p = _attn_layer(h, mod, cond_of_tile, norm1_g[i], w, cache_ckv[:, j],
                                          cache_kpe_pad[:, j], cos_t, sin_t, dims)
            new_ckv.append(ckv_p.reshape(n_prompt, seq, -1))
            new_kpe.append(kpe_p.reshape(n_prompt, seq, -1))
        n2, idx, gate = _peer_route(h, mod, cond_of_tile, norm2_g[i], peer_wq[i].T.astype(BF16),
                                    peer_keys[i].astype(BF16))
        h = _peer_apply(h, n2, idx, gate, mod, cond_of_step, table, i)

    y_prompt = h[:n_p_rows].reshape(n_prompt, seq, d)
    y_sample = h[n_p_rows:].reshape(n_dec, dec_seq, d)
    return (y_prompt, y_sample, jnp.stack(new_ckv, axis=1), jnp.stack(new_kpe, axis=1))
```

```python
import functools

import jax
import jax.numpy as jnp
from jax import lax
from jax.experimental import pallas as pl
from jax.experimental.pallas import tpu as pltpu

F32 = jnp.float32
BF16 = jnp.bfloat16

D_MODEL = 1024
GRID_W = 64
CONV_K = 31
MLA_HEADS = 16
QK_NOPE = 64
QK_ROPE = 32
V_HEAD = 64
QK_DIM = QK_NOPE + QK_ROPE
ROPE_THETA = 10000.0
PEER_HEADS = 8
N_KEYS = 128
PEER_TOPK = 16
EPS = 1e-6

LANES = 128
SUBLANES = 8
TM = 256
HALO = 16
TB = 8
ROWS_PER_TOKEN = PEER_HEADS * PEER_TOPK
VMEM_LIMIT = 48 * 1024 * 1024


def _cparams(n_grid):
    return pltpu.CompilerParams(
        dimension_semantics=("arbitrary",) * n_grid, vmem_limit_bytes=VMEM_LIMIT)


def _rms(x, g, denom=None):
    n = x.shape[-1] if denom is None else denom
    ms = jnp.sum(x * x, axis=-1, keepdims=True) / float(n)
    return x * lax.rsqrt(ms + EPS) * g


def _modulate(h, g, shift, scale):
    return _rms(h, g) * (1.0 + scale) + shift


def _gelu(x):
    return 0.5 * x * (1.0 + lax.erf(x * (0.5 ** 0.5)))


def _mod_slice(m, k):
    return m[:, k * D_MODEL:(k + 1) * D_MODEL]


def _mod_kernel(c_ref, w_ref, b_ref, o_ref):
    c = c_ref[...]
    a = (c * jax.nn.sigmoid(c)).astype(BF16)
    o_ref[...] = jnp.dot(a, w_ref[...].astype(BF16), preferred_element_type=F32) + b_ref[...]


def _modulation_all(cond8, mod_w, mod_b):
    depth, d, n6 = mod_w.shape
    tn = 1536
    return pl.pallas_call(
        _mod_kernel,
        grid=(depth, n6 // tn),
        in_specs=[
            pl.BlockSpec((8, d), lambda l, n: (0, 0)),
            pl.BlockSpec((None, d, tn), lambda l, n: (l, 0, n)),
            pl.BlockSpec((None, 1, tn), lambda l, n: (l, 0, n)),
        ],
        out_specs=pl.BlockSpec((None, 8, tn), lambda l, n: (l, 0, n)),
        out_shape=jax.ShapeDtypeStruct((depth, 8, n6), F32),
        compiler_params=_cparams(2),
        name="modulation",
    )(cond8, mod_w, mod_b.reshape(depth, 1, n6))


def _conv_in_kernel(cond_ref, h_ref, m_ref, g_ref, w_ref, b_ref, o_ref):
    del cond_ref
    m = m_ref[...]
    n = _modulate(h_ref[...], g_ref[...], _mod_slice(m, 0), _mod_slice(m, 1))
    hh = jnp.dot(n.astype(BF16), w_ref[...], preferred_element_type=F32) + b_ref[...]
    o_ref[...] = hh[:, :D_MODEL] * jax.nn.sigmoid(hh[:, D_MODEL:])


def _conv_out_kernel(first_ref, last_ref, cond_ref, cur_ref, prev_ref, next_ref, dw_ref, dwb_ref, ng_ref,
                     w_ref, b_ref, h_ref, m_ref, o_ref, pad_ref, conv_ref):
    del cond_ref
    i = pl.program_id(0)
    keep_prev = jnp.where(first_ref[i] == 0, 1.0, 0.0)
    keep_next = jnp.where(last_ref[i] == 0, 1.0, 0.0)
    pad_ref[0:HALO, :] = prev_ref[...] * keep_prev
    pad_ref[HALO:HALO + TM, :] = cur_ref[...]
    pad_ref[HALO + TM:HALO + TM + HALO, :] = next_ref[...] * keep_next
    base = HALO - CONV_K // 2
    for c in range(D_MODEL // LANES):
        cols = slice(c * LANES, (c + 1) * LANES)
        acc = jnp.zeros((TM, LANES), F32) + dwb_ref[:, cols]
        for k in range(CONV_K):
            acc = acc + dw_ref[k:k + 1, cols] * pad_ref[base + k:base + k + TM, cols]
        conv_ref[:, cols] = acc
    y = _rms(conv_ref[...], ng_ref[...])
    y = y * jax.nn.sigmoid(y)
    out = jnp.dot(y.astype(BF16), w_ref[...], preferred_element_type=F32) + b_ref[...]
    o_ref[...] = h_ref[...] + _mod_slice(m_ref[...], 2) * out


def _conv_layer(h, mod, cond_of_tile, seq_first, seq_last, g, w_in, b_in, dw, dw_b, ng, w_out, b_out):
    n_tok, d = h.shape
    nt = n_tok // TM
    hb = TM // HALO
    tile = lambda i, *_: (i, 0)
    full = lambda i, *_: (0, 0)
    modmap = lambda i, cond, *_: (cond[i], 0, 0)
    glu = pl.pallas_call(
        _conv_in_kernel,
        grid_spec=pltpu.PrefetchScalarGridSpec(
            num_scalar_prefetch=1, grid=(nt,),
            in_specs=[
                pl.BlockSpec((TM, d), lambda i, cond: (i, 0)),
                pl.BlockSpec((None, 1, 6 * d), lambda i, cond: (cond[i], 0, 0)),
                pl.BlockSpec((1, d), lambda i, cond: (0, 0)),
                pl.BlockSpec((d, 2 * d), lambda i, cond: (0, 0)),
                pl.BlockSpec((1, 2 * d), lambda i, cond: (0, 0)),
            ],
            out_specs=pl.BlockSpec((TM, d), lambda i, cond: (i, 0))),
        out_shape=jax.ShapeDtypeStruct((n_tok, d), F32),
        compiler_params=_cparams(1),
        name="conv_in",
    )(cond_of_tile, h, mod, g.reshape(1, d), w_in, b_in.reshape(1, 2 * d))

    n_hb = n_tok // HALO
    dw_pad = jnp.zeros((32, d), F32).at[:CONV_K].set(dw)
    return pl.pallas_call(
        _conv_out_kernel,
        grid_spec=pltpu.PrefetchScalarGridSpec(
            num_scalar_prefetch=3, grid=(nt,),
            in_specs=[
                pl.BlockSpec((TM, d), lambda i, f, l, cond: (i, 0)),
                pl.BlockSpec((HALO, d), lambda i, f, l, cond: (jnp.maximum(i * hb - 1, 0), 0)),
                pl.BlockSpec((HALO, d), lambda i, f, l, cond: (jnp.minimum((i + 1) * hb, n_hb - 1), 0)),
                pl.BlockSpec((32, d), lambda i, f, l, cond: (0, 0)),
                pl.BlockSpec((1, d), lambda i, f, l, cond: (0, 0)),
                pl.BlockSpec((1, d), lambda i, f, l, cond: (0, 0)),
                pl.BlockSpec((d, d), lambda i, f, l, cond: (0, 0)),
                pl.BlockSpec((1, d), lambda i, f, l, cond: (0, 0)),
                pl.BlockSpec((TM, d), lambda i, f, l, cond: (i, 0)),
                pl.BlockSpec((None, 1, 6 * d), lambda i, f, l, cond: (cond[i], 0, 0)),
            ],
            out_specs=pl.BlockSpec((TM, d), lambda i, f, l, cond: (i, 0)),
            scratch_shapes=[pltpu.VMEM((TM + 2 * HALO, d), F32), pltpu.VMEM((TM, d), F32)]),
        out_shape=jax.ShapeDtypeStruct((n_tok, d), F32),
        compiler_params=_cparams(1),
        name="conv_out",
    )(seq_first, seq_last, cond_of_tile, glu, glu, glu, dw_pad, dw_b.reshape(1, d), ng.reshape(1, d),
      w_out, b_out.reshape(1, d), h, mod)


def _rope(y, cos, sin):
    lane = lax.broadcasted_iota(jnp.int32, y.shape, 1)
    nxt = pltpu.roll(y, LANES - 1, 1)
    prv = pltpu.roll(y, 1, 1)
    partner = jnp.where((lane & 1) == 0, nxt, prv)
    return y * cos + partner * sin


def _attn_latent_kernel(cond_ref, h_ref, m_ref, g_ref, qaw_ref, qan_ref, kvw_ref, kvn_ref, kpew_ref,
                        qlat_ref, ckv_ref, kpe_ref):
    del cond_ref
    m = m_ref[...]
    n = _modulate(h_ref[...], g_ref[...], _mod_slice(m, 0), _mod_slice(m, 1)).astype(BF16)
    qa = jnp.dot(n, qaw_ref[...], preferred_element_type=F32)
    qlat_ref[...] = _rms(qa, qan_ref[...]).astype(BF16)
    kv = jnp.dot(n, kvw_ref[...], preferred_element_type=F32)
    ckv_ref[...] = _rms(kv, kvn_ref[...])
    kpe_ref[...] = jnp.dot(n, kpew_ref[...], preferred_element_type=F32)


def _q_kernel(qlat_ref, w_ref, qn_ref, cos_ref, sin_ref, o_ref):
    q = jnp.dot(qlat_ref[...], w_ref[...], preferred_element_type=F32)
    cos = cos_ref[...]
    sin = sin_ref[...]
    qn = qn_ref[...]
    for hd in range(MLA_HEADS):
        cols = slice(hd * LANES, (hd + 1) * LANES)
        y = _rms(q[:, cols], qn, denom=QK_DIM)
        o_ref[:, cols] = _rope(y, cos, sin).astype(BF16)


def _kv_kernel(ckv_ref, kpe_ref, wk_ref, wv_ref, kn_ref, cos_ref, sin_ref, k_ref, v_ref):
    c = ckv_ref[...].astype(BF16)
    kf = jnp.dot(c, wk_ref[...], preferred_element_type=F32)
    v_ref[...] = jnp.dot(c, wv_ref[...], preferred_element_type=F32).astype(BF16)
    kpe = kpe_ref[...]
    cos = cos_ref[...]
    sin = sin_ref[...]
    kn = kn_ref[...]
    for hd in range(MLA_HEADS):
        cols = slice(hd * LANES, (hd + 1) * LANES)
        y = _rms(kf[:, cols] + kpe, kn, denom=QK_DIM)
        k_ref[:, cols] = _rope(y, cos, sin).astype(BF16)


def _attn_kernel(q_ref, k_ref, v_ref, o_ref):
    scale = float(QK_DIM) ** -0.5
    for hh in range(2):
        q = q_ref[:, hh * LANES:(hh + 1) * LANES]
        k = k_ref[:, hh * LANES:(hh + 1) * LANES]
        s = lax.dot_general(q, k, (((1,), (1,)), ((), ())), preferred_element_type=F32) * scale
        p = jnp.exp(s - jnp.max(s, axis=-1, keepdims=True))
        l = jnp.sum(p, axis=-1, keepdims=True)
        o = jnp.dot(p.astype(BF16), v_ref[:, hh * V_HEAD:(hh + 1) * V_HEAD], preferred_element_type=F32)
        o_ref[:, hh * V_HEAD:(hh + 1) * V_HEAD] = (o / l).astype(BF16)


def _attn_out_kernel(cond_ref, op_ref, os_ref, w_ref, h_ref, m_ref, o_ref, *, n_prompt_tiles):
    del cond_ref
    i = pl.program_id(0)
    o = jnp.where(i < n_prompt_tiles, op_ref[...], os_ref[...])
    out = jnp.dot(o, w_ref[...], preferred_element_type=F32)
    o_ref[...] = h_ref[...] + _mod_slice(m_ref[...], 2) * out


def _attention(q, k, v, n_batch, sq, sk, q_tile_offset):
    nq = sq // TM
    hp = MLA_HEADS // 2
    return pl.pallas_call(
        _attn_kernel,
        grid=(n_batch, hp, nq),
        in_specs=[
            pl.BlockSpec((TM, 2 * LANES), lambda b, h, qi: (q_tile_offset + b * nq + qi, h)),
            pl.BlockSpec((sk, 2 * LANES), lambda b, h, qi: (b, h)),
            pl.BlockSpec((sk, 2 * V_HEAD), lambda b, h, qi: (b, h)),
        ],
        out_specs=pl.BlockSpec((TM, 2 * V_HEAD), lambda b, h, qi: (b * nq + qi, h)),
        out_shape=jax.ShapeDtypeStruct((n_batch * sq, MLA_HEADS * V_HEAD), BF16),
        compiler_params=_cparams(3),
        name="attention",
    )(q, k, v)


def _kv_expand(ckv, kpe, wk, wv, kn, cos_t, sin_t, table_block):
    n_rows = ckv.shape[0]
    nt = n_rows // TM
    return pl.pallas_call(
        _kv_kernel,
        grid=(nt,),
        in_specs=[
            pl.BlockSpec((TM, ckv.shape[1]), lambda i: (i, 0)),
            pl.BlockSpec((TM, LANES), lambda i: (i, 0)),
            pl.BlockSpec(wk.shape, lambda i: (0, 0)),
            pl.BlockSpec(wv.shape, lambda i: (0, 0)),
            pl.BlockSpec((1, LANES), lambda i: (0, 0)),
            pl.BlockSpec((TM, LANES), lambda i: (table_block(i), 0)),
            pl.BlockSpec((TM, LANES), lambda i: (table_block(i), 0)),
        ],
        out_specs=[pl.BlockSpec((TM, MLA_HEADS * LANES), lambda i: (i, 0)),
                   pl.BlockSpec((TM, MLA_HEADS * V_HEAD), lambda i: (i, 0))],
        out_shape=[jax.ShapeDtypeStruct((n_rows, MLA_HEADS * LANES), BF16),
                   jax.ShapeDtypeStruct((n_rows, MLA_HEADS * V_HEAD), BF16)],
        compiler_params=_cparams(1),
        name="kv_expand",
    )(ckv, kpe, wk, wv, kn, cos_t, sin_t)


def _attn_layer(h, mod, cond_of_tile, g, w, cache_ckv, cache_kpe, cos_t, sin_t, dims):
    n_tok, d = h.shape
    nt = n_tok // TM
    n_prompt, seq, n_dec, dec_seq, past = dims
    npt = n_prompt * seq // TM
    tps = dec_seq // TM
    q_lora = w["q_a_w"].shape[1]
    kv_lora = w["kv_w"].shape[1]

    qlat, ckv, kpe = pl.pallas_call(
        _attn_latent_kernel,
        grid_spec=pltpu.PrefetchScalarGridSpec(
            num_scalar_prefetch=1, grid=(nt,),
            in_specs=[
                pl.BlockSpec((TM, d), lambda i, cond: (i, 0)),
                pl.BlockSpec((None, 1, 6 * d), lambda i, cond: (cond[i], 0, 0)),
                pl.BlockSpec((1, d), lambda i, cond: (0, 0)),
                pl.BlockSpec((d, q_lora), lambda i, cond: (0, 0)),
                pl.BlockSpec((1, q_lora), lambda i, cond: (0, 0)),
                pl.BlockSpec((d, kv_lora), lambda i, cond: (0, 0)),
                pl.BlockSpec((1, kv_lora), lambda i, cond: (0, 0)),
                pl.BlockSpec((d, LANES), lambda i, cond: (0, 0)),
            ],
            out_specs=[pl.BlockSpec((TM, q_lora), lambda i, cond: (i, 0)),
                       pl.BlockSpec((TM, kv_lora), lambda i, cond: (i, 0)),
                       pl.BlockSpec((TM, LANES), lambda i, cond: (i, 0))]),
        out_shape=[jax.ShapeDtypeStruct((n_tok, q_lora), BF16),
                   jax.ShapeDtypeStruct((n_tok, kv_lora), F32),
                   jax.ShapeDtypeStruct((n_tok, LANES), F32)],
        compiler_params=_cparams(1),
        name="attn_latent",
    )(cond_of_tile, h, mod, g.reshape(1, d), w["q_a_w"], w["q_a_norm"], w["kv_w"], w["kv_norm"], w["kpe_w"])

    q_table = lambda i: jnp.where(i < npt, 0, 1 + (i - npt) % tps)
    q = pl.pallas_call(
        _q_kernel,
        grid=(nt,),
        in_specs=[
            pl.BlockSpec((TM, q_lora), lambda i: (i, 0)),
            pl.BlockSpec(w["q_b_w"].shape, lambda i: (0, 0)),
            pl.BlockSpec((1, LANES), lambda i: (0, 0)),
            pl.BlockSpec((TM, LANES), lambda i: (q_table(i), 0)),
            pl.BlockSpec((TM, LANES), lambda i: (q_table(i), 0)),
        ],
        out_specs=pl.BlockSpec((TM, MLA_HEADS * LANES), lambda i: (i, 0)),
        out_shape=jax.ShapeDtypeStruct((n_tok, MLA_HEADS * LANES), BF16),
        compiler_params=_cparams(1),
        name="q_expand",
    )(qlat, w["q_b_w"], w["q_norm"], cos_t, sin_t)

    n_p_rows = n_prompt * seq
    ckv_p, kpe_p = ckv[:n_p_rows], kpe[:n_p_rows]
    k_p, v_p = _kv_expand(ckv_p, kpe_p, w["k_w"], w["v_w"], w["k_norm"], cos_t, sin_t, lambda i: 0)
    ckv_s = jnp.concatenate([cache_ckv, ckv[n_p_rows:].reshape(n_dec, dec_seq, kv_lora)], axis=1)
    kpe_s = jnp.concatenate([cache_kpe, kpe[n_p_rows:].reshape(n_dec, dec_seq, LANES)], axis=1)
    sk = past + dec_seq
    tpk = sk // TM
    pc = past // TM
    k_s, v_s = _kv_expand(ckv_s.reshape(n_dec * sk, kv_lora), kpe_s.reshape(n_dec * sk, LANES),
                          w["k_w"], w["v_w"], w["k_norm"], cos_t, sin_t,
                          lambda i: jnp.where(i % tpk < pc, 0, 1 + i % tpk - pc))

    o_p = _attention(q, k_p, v_p, n_prompt, seq, seq, 0)
    o_s = _attention(q, k_s, v_s, n_dec, dec_seq, sk, npt)

    h_new = pl.pallas_call(
        functools.partial(_attn_out_kernel, n_prompt_tiles=npt),
        grid_spec=pltpu.PrefetchScalarGridSpec(
            num_scalar_prefetch=1, grid=(nt,),
            in_specs=[
                pl.BlockSpec((TM, d), lambda i, cond: (jnp.minimum(i, npt - 1), 0)),
                pl.BlockSpec((TM, d), lambda i, cond: (jnp.maximum(i - npt, 0), 0)),
                pl.BlockSpec((d, d), lambda i, cond: (0, 0)),
                pl.BlockSpec((TM, d), lambda i, cond: (i, 0)),
                pl.BlockSpec((None, 1, 6 * d), lambda i, cond: (cond[i], 0, 0)),
            ],
            out_specs=pl.BlockSpec((TM, d), lambda i, cond: (i, 0))),
        out_shape=jax.ShapeDtypeStruct((n_tok, d), F32),
        compiler_params=_cparams(1),
        name="attn_out",
    )(cond_of_tile, o_p, o_s, w["o_w"], h, mod)
    return h_new, ckv_p, kpe_p[:, QK_NOPE:QK_NOPE + QK_ROPE]


def _topk_rows(s, payload, k):
    n = s.shape[0]
    row = lax.broadcasted_iota(jnp.int32, s.shape, 0).astype(F32)
    vals, pays = [], []
    for _ in range(k):
        m = jnp.max(s, axis=0, keepdims=True)
        pos = jnp.min(jnp.where(s == m, row, float(n)), axis=0, keepdims=True)
        hit = row == pos
        if payload is None:
            pays.append(pos)
        else:
            pays.append(jnp.max(jnp.where(hit, payload, -1.0), axis=0, keepdims=True))
        vals.append(m)
        s = jnp.where(hit, -jnp.inf, s)
    return jnp.concatenate(vals, axis=0), jnp.concatenate(pays, axis=0)


def _peer_route_kernel(cond_ref, h_ref, m_ref, g_ref, wqt_ref, keys_ref, n_ref, idx_ref, gate_ref,
                       qt_ref, sv_ref, si_ref, e_ref, gt_ref):
    del cond_ref
    m = m_ref[...]
    n = _modulate(h_ref[...], g_ref[...], _mod_slice(m, 3), _mod_slice(m, 4))
    n_ref[...] = n
    qt_ref[...] = lax.dot_general(wqt_ref[...], n.astype(BF16), (((1,), (1,)), ((), ())),
                                  preferred_element_type=F32)

    def sub_key(hp, carry):
        q = qt_ref[pl.ds(pl.multiple_of(hp * N_KEYS, N_KEYS), N_KEYS), :].astype(BF16)
        s = jnp.dot(keys_ref[hp % 2], q, preferred_element_type=F32)
        vals, pos = _topk_rows(s, None, PEER_TOPK)
        sv_ref[hp] = vals
        si_ref[hp] = pos
        return carry

    lax.fori_loop(0, 2 * PEER_HEADS, sub_key, 0)

    def head(hd, carry):
        sv0, sv1 = sv_ref[2 * hd], sv_ref[2 * hd + 1]
        si0, si1 = si_ref[2 * hd], si_ref[2 * hd + 1]
        cand = jnp.concatenate([sv0[a:a + 1] + sv1 for a in range(PEER_TOPK)], axis=0)
        cidx = jnp.concatenate([si0[a:a + 1] * float(N_KEYS) + si1 for a in range(PEER_TOPK)], axis=0)
        top_s, top_e = _topk_rows(cand, cidx, PEER_TOPK)
        p = jnp.exp(top_s - top_s[0:1])
        rows = pl.ds(pl.multiple_of(hd * PEER_TOPK, PEER_TOPK), PEER_TOPK)
        gt_ref[rows, :] = p / jnp.sum(p, axis=0, keepdims=True)
        e_ref[rows, :] = top_e
        return carry

    lax.fori_loop(0, PEER_HEADS, head, 0)
    idx_ref[...] = e_ref[...].T.astype(jnp.int32)
    gate_ref[...] = gt_ref[...].T


def _peer_route(h, mod, cond_of_tile, g, wqt, keys):
    n_tok, d = h.shape
    nt = n_tok // TM
    nq = wqt.shape[0]
    return pl.pallas_call(
        _peer_route_kernel,
        grid_spec=pltpu.PrefetchScalarGridSpec(
            num_scalar_prefetch=1, grid=(nt,),
            in_specs=[
                pl.BlockSpec((TM, d), lambda i, cond: (i, 0)),
                pl.BlockSpec((None, 1, 6 * d), lambda i, cond: (cond[i], 0, 0)),
                pl.BlockSpec((1, d), lambda i, cond: (0, 0)),
                pl.BlockSpec((nq, d), lambda i, cond: (0, 0)),
                pl.BlockSpec(keys.shape, lambda i, cond: (0, 0, 0)),
            ],
            out_specs=[pl.BlockSpec((TM, d), lambda i, cond: (i, 0)),
                       pl.BlockSpec((TM, ROWS_PER_TOKEN), lambda i, cond: (i, 0)),
                       pl.BlockSpec((TM, ROWS_PER_TOKEN), lambda i, cond: (i, 0))],
            scratch_shapes=[
                pltpu.VMEM((nq, TM), F32),
                pltpu.VMEM((2 * PEER_HEADS, PEER_TOPK, TM), F32),
                pltpu.VMEM((2 * PEER_HEADS, PEER_TOPK, TM), F32),
                pltpu.VMEM((ROWS_PER_TOKEN, TM), F32),
                pltpu.VMEM((ROWS_PER_TOKEN, TM), F32),
            ]),
        out_shape=[jax.ShapeDtypeStruct((n_tok, d), F32),
                   jax.ShapeDtypeStruct((n_tok, ROWS_PER_TOKEN), jnp.int32),
                   jax.ShapeDtypeStruct((n_tok, ROWS_PER_TOKEN), F32)],
        compiler_params=_cparams(1),
        name="peer_route",
    )(cond_of_tile, h, mod, g.reshape(1, d), wqt, keys)


def _peer_apply_kernel(cond_ref, idx_hbm, gate_ref, n_ref, h_ref, m_ref, tab_all_hbm, o_ref,
                       idx_smem, rows, idx_sem, row_sem, *, layer):
    del cond_ref
    tab_hbm = tab_all_hbm.at[layer]
    i = pl.program_id(0)
    n_steps = pl.num_programs(0)
    n_rows = TB * ROWS_PER_TOKEN
    d = D_MODEL

    def idx_copy(step, slot):
        return pltpu.make_async_copy(idx_hbm.at[step], idx_smem.at[pl.ds(slot * n_rows, n_rows)], idx_sem.at[slot])

    def issue_rows(slot):
        def body(j, carry):
            base = slot * n_rows + j * SUBLANES
            for k in range(SUBLANES):
                e = idx_smem[base + k]
                pltpu.make_async_copy(tab_hbm.at[e], rows.at[slot, j, :, k, :], row_sem.at[slot]).start()
            return carry
        lax.fori_loop(0, n_rows // SUBLANES, body, 0)

    slot = i % 2
    nslot = 1 - slot

    @pl.when(i == 0)
    def _():
        idx_copy(0, 0).start()
        idx_copy(0, 0).wait()
        issue_rows(0)

        @pl.when(n_steps > 1)
        def _():
            idx_copy(1, 1).start()

    @pl.when(i + 1 < n_steps)
    def _():
        idx_copy(i + 1, nslot).wait()
        issue_rows(nslot)

        @pl.when(i + 2 < n_steps)
        def _():
            idx_copy(i + 2, slot).start()

    pltpu.make_async_copy(rows.at[nslot], rows.at[slot], row_sem.at[slot]).wait()

    n_chunks = d // LANES
    groups = ROWS_PER_TOKEN // SUBLANES
    lane = lax.broadcasted_iota(jnp.int32, (ROWS_PER_TOKEN, LANES), 1)

    def chunk(t, c):
        return rows[slot, t * groups:(t + 1) * groups, c].reshape(ROWS_PER_TOKEN, LANES)

    scores = jnp.zeros((ROWS_PER_TOKEN, LANES), F32)
    for t in range(TB):
        acc = jnp.zeros((ROWS_PER_TOKEN, LANES), F32)
        for c in range(n_chunks):
            acc = acc + chunk(t, c) * n_ref[t:t + 1, c * LANES:(c + 1) * LANES]
        scores = jnp.where(lane == t, jnp.sum(acc, axis=-1, keepdims=True), scores)
    act = _gelu(scores.T[0:TB, :])
    wgt = gate_ref[...] * act
    wgt_t = jnp.concatenate([wgt, jnp.zeros((LANES - TB, ROWS_PER_TOKEN), F32)], axis=0).T
    for t in range(TB):
        w_col = jnp.broadcast_to(wgt_t[:, t:t + 1], (ROWS_PER_TOKEN, LANES))
        for c in range(n_chunks):
            cols = slice(c * LANES, (c + 1) * LANES)
            o = jnp.sum(w_col * chunk(t, n_chunks + c), axis=0, keepdims=True)
            o_ref[t:t + 1, cols] = h_ref[t:t + 1, cols] + m_ref[:, 5 * d + c * LANES:5 * d + (c + 1) * LANES] * o


def _peer_apply(h, n, idx, gate, mod, cond_of_step, table, layer):
    n_tok, d = h.shape
    n_steps = n_tok // TB
    idx_steps = idx.reshape(n_steps, TB * ROWS_PER_TOKEN)
    return pl.pallas_call(
        functools.partial(_peer_apply_kernel, layer=layer),
        grid_spec=pltpu.PrefetchScalarGridSpec(
            num_scalar_prefetch=1, grid=(n_steps,),
            in_specs=[
                pl.BlockSpec(memory_space=pl.ANY),
                pl.BlockSpec((TB, ROWS_PER_TOKEN), lambda i, cond: (i, 0)),
                pl.BlockSpec((TB, d), lambda i, cond: (i, 0)),
                pl.BlockSpec((TB, d), lambda i, cond: (i, 0)),
                pl.BlockSpec((None, 1, 6 * d), lambda i, cond: (cond[i], 0, 0)),
                pl.BlockSpec(memory_space=pl.ANY),
            ],
            out_specs=pl.BlockSpec((TB, d), lambda i, cond: (i, 0)),
            scratch_shapes=[
                pltpu.SMEM((2 * TB * ROWS_PER_TOKEN,), jnp.int32),
                pltpu.VMEM((2, TB * ROWS_PER_TOKEN // SUBLANES, 2 * d // LANES, SUBLANES, LANES), F32),
                pltpu.SemaphoreType.DMA((2,)),
                pltpu.SemaphoreType.DMA((2,)),
            ]),
        out_shape=jax.ShapeDtypeStruct((n_tok, d), F32),
        compiler_params=_cparams(1),
        name="peer_apply",
    )(cond_of_step, idx_steps, gate, n, h, mod, table)


def _rope_tables(seq_len):
    rows = seq_len // GRID_W
    row = jnp.repeat(jnp.arange(rows), GRID_W).astype(F32)
    col = jnp.tile(jnp.arange(GRID_W), rows).astype(F32)
    half = QK_ROPE // 2
    inv = ROPE_THETA ** (-(jnp.arange(half // 2, dtype=F32) * 2.0 / half))
    ang = jnp.concatenate([row[:, None] * inv, col[:, None] * inv], axis=-1)
    cos, sin = jnp.cos(ang), jnp.sin(ang)
    cos_pairs = jnp.repeat(cos, 2, axis=-1)
    sin_pairs = jnp.stack([-sin, sin], axis=-1).reshape(seq_len, QK_ROPE)
    cos_t = jnp.ones((TM + seq_len, LANES), F32).at[TM:, QK_NOPE:QK_DIM].set(cos_pairs)
    sin_t = jnp.zeros((TM + seq_len, LANES), F32).at[TM:, QK_NOPE:QK_DIM].set(sin_pairs)
    return cos_t, sin_t


def _pad_heads(w, width):
    k = w.shape[0]
    w3 = w.reshape(k, MLA_HEADS, width)
    return jnp.pad(w3, ((0, 0), (0, 0), (0, LANES - width))).reshape(k, MLA_HEADS * LANES)


def _attn_weights(j, q_a_w, q_a_norm, q_b_w, kv_a_w, kv_a_norm, kv_b_w, q_norm, k_norm, o_w):
    d = q_a_w.shape[1]
    kv_lora = kv_a_norm.shape[1]
    kvb = kv_b_w[j].reshape(kv_lora, MLA_HEADS, QK_NOPE + V_HEAD)
    lane_pad = (0, LANES - QK_DIM)
    return {
        "q_a_w": q_a_w[j].astype(BF16),
        "q_a_norm": q_a_norm[j].reshape(1, -1),
        "q_b_w": _pad_heads(q_b_w[j], QK_DIM).astype(BF16),
        "kv_w": kv_a_w[j][:, :kv_lora].astype(BF16),
        "kv_norm": kv_a_norm[j].reshape(1, -1),
        "kpe_w": jnp.pad(kv_a_w[j][:, kv_lora:], ((0, 0), (QK_NOPE, LANES - QK_DIM))).astype(BF16),
        "k_w": _pad_heads(kvb[:, :, :QK_NOPE].reshape(kv_lora, MLA_HEADS * QK_NOPE), QK_NOPE).astype(BF16),
        "v_w": kvb[:, :, QK_NOPE:].reshape(kv_lora, MLA_HEADS * V_HEAD).astype(BF16),
        "q_norm": jnp.pad(q_norm[j], lane_pad).reshape(1, LANES),
        "k_norm": jnp.pad(k_norm[j], lane_pad).reshape(1, LANES),
        "o_w": o_w[j].astype(BF16),
    }


def kernel(x_prompt, x_sample, cache_ckv, cache_kpe, c, c_ctx, mod_w, mod_b, norm1_g, norm2_g, conv_w_in, conv_b_in, conv_dw, conv_dw_b, conv_norm_g, conv_w_out, conv_b_out, q_a_w, q_a_norm, q_b_w, kv_a_w, kv_a_norm, kv_b_w, q_norm, k_norm, o_w, peer_wq, peer_keys, peer_u, peer_v):
    n_prompt, seq, d = x_prompt.shape
    n_dec, dec_seq, _ = x_sample.shape
    past = cache_ckv.shape[2]
    depth = mod_w.shape[0]
    assert d == D_MODEL and seq % TM == 0 and dec_seq % TM == 0 and past % TM == 0
    assert 1 + n_dec <= 8
    dims = (n_prompt, seq, n_dec, dec_seq, past)

    n_p_rows = n_prompt * seq
    h = jnp.concatenate([x_prompt.reshape(n_p_rows, d), x_sample.reshape(n_dec * dec_seq, d)], axis=0)
    n_tok = h.shape[0]

    tiles_p, tps = n_p_rows // TM, dec_seq // TM
    tile = jnp.arange(n_tok // TM, dtype=jnp.int32)
    in_p = tile < tiles_p
    cond_of_tile = jnp.where(in_p, 0, 1 + (tile - tiles_p) // tps).astype(jnp.int32)
    pos_p, pos_s = tile % (seq // TM), (tile - tiles_p) % tps
    seq_first = jnp.where(in_p, pos_p == 0, pos_s == 0).astype(jnp.int32)
    seq_last = jnp.where(in_p, pos_p == seq // TM - 1, pos_s == tps - 1).astype(jnp.int32)
    cond_of_step = jnp.repeat(cond_of_tile, TM // TB)

    cond8 = jnp.zeros((8, d), F32).at[0].set(c_ctx).at[1:1 + n_dec].set(c)
    mod_all = _modulation_all(cond8, mod_w, mod_b)
    table = jnp.concatenate([peer_u, peer_v], axis=-1).reshape(depth, peer_u.shape[1], 2 * d // LANES, LANES)
    cos_t, sin_t = _rope_tables(dec_seq)
    cache_kpe_pad = jnp.pad(cache_kpe, ((0, 0), (0, 0), (0, 0), (QK_NOPE, LANES - QK_DIM)))

    new_ckv, new_kpe = [], []
    for i in range(depth):
        mod = mod_all[i].reshape(8, 1, 6 * d)
        j = i // 2
        if i % 2 == 0:
            h = _conv_layer(h, mod, cond_of_tile, seq_first, seq_last, norm1_g[i],
                            conv_w_in[j].astype(BF16), conv_b_in[j], conv_dw[j], conv_dw_b[j],
                            conv_norm_g[j], conv_w_out[j].astype(BF16), conv_b_out[j])
        else:
            w = _attn_weights(j, q_a_w, q_a_norm, q_b_w, kv_a_w, kv_a_norm, kv_b_w, q_norm, k_norm, o_w)
            h, ckv_p, kpe_p = _attn_layer(h, mod, cond_of_tile, norm1_g[i], w, cache_ckv[:, j],
                                          cache_kpe_pad[:, j], cos_t, sin_t, dims)
            new_ckv.append(ckv_p.reshape(n_prompt, seq, -1))
            new_kpe.append(kpe_p.reshape(n_prompt, seq, -1))
        n2, idx, gate = _peer_route(h, mod, cond_of_tile, norm2_g[i], peer_wq[i].T.astype(BF16),
                                    peer_keys[i].astype(BF16))
        h = _peer_apply(h, n2, idx, gate, mod, cond_of_step, table, i)

    y_prompt = h[:n_p_rows].reshape(n_prompt, seq, d)
    y_sample = h[n_p_rows:].reshape(n_dec, dec_seq, d)
    return (y_prompt, y_sample, jnp.stack(new_ckv, axis=1), jnp.stack(new_kpe, axis=1))
```

```python
import functools

import jax
import jax.numpy as jnp
from jax import lax
from jax.experimental import pallas as pl
from jax.experimental.pallas import tpu as pltpu

F32 = jnp.float32
BF16 = jnp.bfloat16

D_MODEL = 1024
GRID_W = 64
CONV_K = 31
MLA_HEADS = 16
QK_NOPE = 64
QK_ROPE = 32
V_HEAD = 64
QK_DIM = QK_NOPE + QK_ROPE
ROPE_THETA = 10000.0
PEER_HEADS = 8
N_KEYS = 128
PEER_TOPK = 16
EPS = 1e-6

LANES = 128
SUBLANES = 8
TM = 256
HALO = 16
TB = 8
TD = 512
EB = 1024
DENSE_VMEM_LIMIT = 56 * 1024 * 1024
F32_MIN_NORMAL = 1.1754943508222875e-38
ROWS_PER_TOKEN = PEER_HEADS * PEER_TOPK
VMEM_LIMIT = 48 * 1024 * 1024


def _cparams(n_grid):
    return pltpu.CompilerParams(
        dimension_semantics=("arbitrary",) * n_grid, vmem_limit_bytes=VMEM_LIMIT)


def _rms(x, g, denom=None):
    n = x.shape[-1] if denom is None else denom
    ms = jnp.sum(x * x, axis=-1, keepdims=True) / float(n)
    return x * lax.rsqrt(ms + EPS) * g


def _modulate(h, g, shift, scale):
    return _rms(h, g) * (1.0 + scale) + shift


def _gelu(x):
    return 0.5 * x * (1.0 + lax.erf(x * (0.5 ** 0.5)))


def _mod_slice(m, k):
    return m[:, k * D_MODEL:(k + 1) * D_MODEL]


def _mod_kernel(c_ref, w_ref, b_ref, o_ref):
    c = c_ref[...]
    a = (c * jax.nn.sigmoid(c)).astype(BF16)
    o_ref[...] = jnp.dot(a, w_ref[...].astype(BF16), preferred_element_type=F32) + b_ref[...]


def _modulation_all(cond8, mod_w, mod_b):
    depth, d, n6 = mod_w.shape
    tn = 1536
    return pl.pallas_call(
        _mod_kernel,
        grid=(depth, n6 // tn),
        in_specs=[
            pl.BlockSpec((8, d), lambda l, n: (0, 0)),
            pl.BlockSpec((None, d, tn), lambda l, n: (l, 0, n)),
            pl.BlockSpec((None, 1, tn), lambda l, n: (l, 0, n)),
        ],
        out_specs=pl.BlockSpec((None, 8, tn), lambda l, n: (l, 0, n)),
        out_shape=jax.ShapeDtypeStruct((depth, 8, n6), F32),
        compiler_params=_cparams(2),
        name="modulation",
    )(cond8, mod_w, mod_b.reshape(depth, 1, n6))


def _conv_in_kernel(cond_ref, h_ref, m_ref, g_ref, w_ref, b_ref, o_ref):
    del cond_ref
    m = m_ref[...]
    n = _modulate(h_ref[...], g_ref[...], _mod_slice(m, 0), _mod_slice(m, 1))
    hh = jnp.dot(n.astype(BF16), w_ref[...], preferred_element_type=F32) + b_ref[...]
    o_ref[...] = hh[:, :D_MODEL] * jax.nn.sigmoid(hh[:, D_MODEL:])


def _conv_out_kernel(first_ref, last_ref, cond_ref, cur_ref, prev_ref, next_ref, dw_ref, dwb_ref, ng_ref,
                     w_ref, b_ref, h_ref, m_ref, o_ref, pad_ref, conv_ref):
    del cond_ref
    i = pl.program_id(0)
    keep_prev = jnp.where(first_ref[i] == 0, 1.0, 0.0)
    keep_next = jnp.where(last_ref[i] == 0, 1.0, 0.0)
    pad_ref[0:HALO, :] = prev_ref[...] * keep_prev
    pad_ref[HALO:HALO + TM, :] = cur_ref[...]
    pad_ref[HALO + TM:HALO + TM + HALO, :] = next_ref[...] * keep_next
    base = HALO - CONV_K // 2
    for c in range(D_MODEL // LANES):
        cols = slice(c * LANES, (c + 1) * LANES)
        acc = jnp.zeros((TM, LANES), F32) + dwb_ref[:, cols]
        for k in range(CONV_K):
            acc = acc + dw_ref[k:k + 1, cols] * pad_ref[base + k:base + k + TM, cols]
        conv_ref[:, cols] = acc
    y = _rms(conv_ref[...], ng_ref[...])
    y = y * jax.nn.sigmoid(y)
    out = jnp.dot(y.astype(BF16), w_ref[...], preferred_element_type=F32) + b_ref[...]
    o_ref[...] = h_ref[...] + _mod_slice(m_ref[...], 2) * out


def _conv_layer(h, mod, cond_of_tile, seq_first, seq_last, g, w_in, b_in, dw, dw_b, ng, w_out, b_out):
    n_tok, d = h.shape
    nt = n_tok // TM
    hb = TM // HALO
    tile = lambda i, *_: (i, 0)
    full = lambda i, *_: (0, 0)
    modmap = lambda i, cond, *_: (cond[i], 0, 0)
    glu = pl.pallas_call(
        _conv_in_kernel,
        grid_spec=pltpu.PrefetchScalarGridSpec(
            num_scalar_prefetch=1, grid=(nt,),
            in_specs=[
                pl.BlockSpec((TM, d), lambda i, cond: (i, 0)),
                pl.BlockSpec((None, 1, 6 * d), lambda i, cond: (cond[i], 0, 0)),
                pl.BlockSpec((1, d), lambda i, cond: (0, 0)),
                pl.BlockSpec((d, 2 * d), lambda i, cond: (0, 0)),
                pl.BlockSpec((1, 2 * d), lambda i, cond: (0, 0)),
            ],
            out_specs=pl.BlockSpec((TM, d), lambda i, cond: (i, 0))),
        out_shape=jax.ShapeDtypeStruct((n_tok, d), F32),
        compiler_params=_cparams(1),
        name="conv_in",
    )(cond_of_tile, h, mod, g.reshape(1, d), w_in, b_in.reshape(1, 2 * d))

    n_hb = n_tok // HALO
    dw_pad = jnp.zeros((32, d), F32).at[:CONV_K].set(dw)
    return pl.pallas_call(
        _conv_out_kernel,
        grid_spec=pltpu.PrefetchScalarGridSpec(
            num_scalar_prefetch=3, grid=(nt,),
            in_specs=[
                pl.BlockSpec((TM, d), lambda i, f, l, cond: (i, 0)),
                pl.BlockSpec((HALO, d), lambda i, f, l, cond: (jnp.maximum(i * hb - 1, 0), 0)),
                pl.BlockSpec((HALO, d), lambda i, f, l, cond: (jnp.minimum((i + 1) * hb, n_hb - 1), 0)),
                pl.BlockSpec((32, d), lambda i, f, l, cond: (0, 0)),
                pl.BlockSpec((1, d), lambda i, f, l, cond: (0, 0)),
                pl.BlockSpec((1, d), lambda i, f, l, cond: (0, 0)),
                pl.BlockSpec((d, d), lambda i, f, l, cond: (0, 0)),
                pl.BlockSpec((1, d), lambda i, f, l, cond: (0, 0)),
                pl.BlockSpec((TM, d), lambda i, f, l, cond: (i, 0)),
                pl.BlockSpec((None, 1, 6 * d), lambda i, f, l, cond: (cond[i], 0, 0)),
            ],
            out_specs=pl.BlockSpec((TM, d), lambda i, f, l, cond: (i, 0)),
            scratch_shapes=[pltpu.VMEM((TM + 2 * HALO, d), F32), pltpu.VMEM((TM, d), F32)]),
        out_shape=jax.ShapeDtypeStruct((n_tok, d), F32),
        compiler_params=_cparams(1),
        name="conv_out",
    )(seq_first, seq_last, cond_of_tile, glu, glu, glu, dw_pad, dw_b.reshape(1, d), ng.reshape(1, d),
      w_out, b_out.reshape(1, d), h, mod)


def _rope(y, cos, sin):
    lane = lax.broadcasted_iota(jnp.int32, y.shape, 1)
    nxt = pltpu.roll(y, LANES - 1, 1)
    prv = pltpu.roll(y, 1, 1)
    partner = jnp.where((lane & 1) == 0, nxt, prv)
    return y * cos + partner * sin


def _attn_latent_kernel(cond_ref, h_ref, m_ref, g_ref, qaw_ref, qan_ref, kvw_ref, kvn_ref, kpew_ref,
                        qlat_ref, ckv_ref, kpe_ref):
    del cond_ref
    m = m_ref[...]
    n = _modulate(h_ref[...], g_ref[...], _mod_slice(m, 0), _mod_slice(m, 1)).astype(BF16)
    qa = jnp.dot(n, qaw_ref[...], preferred_element_type=F32)
    qlat_ref[...] = _rms(qa, qan_ref[...]).astype(BF16)
    kv = jnp.dot(n, kvw_ref[...], preferred_element_type=F32)
    ckv_ref[...] = _rms(kv, kvn_ref[...])
    kpe_ref[...] = jnp.dot(n, kpew_ref[...], preferred_element_type=F32)


def _q_kernel(qlat_ref, w_ref, qn_ref, cos_ref, sin_ref, o_ref):
    q = jnp.dot(qlat_ref[...], w_ref[...], preferred_element_type=F32)
    cos = cos_ref[...]
    sin = sin_ref[...]
    qn = qn_ref[...]
    for hd in range(MLA_HEADS):
        cols = slice(hd * LANES, (hd + 1) * LANES)
        y = _rms(q[:, cols], qn, denom=QK_DIM)
        o_ref[:, cols] = _rope(y, cos, sin).astype(BF16)


def _kv_kernel(ckv_ref, kpe_ref, wk_ref, wv_ref, kn_ref, cos_ref, sin_ref, k_ref, v_ref):
    c = ckv_ref[...].astype(BF16)
    kf = jnp.dot(c, wk_ref[...], preferred_element_type=F32)
    v_ref[...] = jnp.dot(c, wv_ref[...], preferred_element_type=F32).astype(BF16)
    kpe = kpe_ref[...]
    cos = cos_ref[...]
    sin = sin_ref[...]
    kn = kn_ref[...]
    for hd in range(MLA_HEADS):
        cols = slice(hd * LANES, (hd + 1) * LANES)
        y = _rms(kf[:, cols] + kpe, kn, denom=QK_DIM)
        k_ref[:, cols] = _rope(y, cos, sin).astype(BF16)


def _attn_kernel(q_ref, k_ref, v_ref, o_ref):
    scale = float(QK_DIM) ** -0.5
    for hh in range(2):
        q = q_ref[:, hh * LANES:(hh + 1) * LANES]
        k = k_ref[:, hh * LANES:(hh + 1) * LANES]
        s = lax.dot_general(q, k, (((1,), (1,)), ((), ())), preferred_element_type=F32) * scale
        p = jnp.exp(s - jnp.max(s, axis=-1, keepdims=True))
        l = jnp.sum(p, axis=-1, keepdims=True)
        o = jnp.dot(p.astype(BF16), v_ref[:, hh * V_HEAD:(hh + 1) * V_HEAD], preferred_element_type=F32)
        o_ref[:, hh * V_HEAD:(hh + 1) * V_HEAD] = (o / l).astype(BF16)


def _attn_out_kernel(cond_ref, op_ref, os_ref, w_ref, h_ref, m_ref, o_ref, *, n_prompt_tiles):
    del cond_ref
    i = pl.program_id(0)
    o = jnp.where(i < n_prompt_tiles, op_ref[...], os_ref[...])
    out = jnp.dot(o, w_ref[...], preferred_element_type=F32)
    o_ref[...] = h_ref[...] + _mod_slice(m_ref[...], 2) * out


def _attention(q, k, v, n_batch, sq, sk, q_tile_offset):
    nq = sq // TM
    hp = MLA_HEADS // 2
    return pl.pallas_call(
        _attn_kernel,
        grid=(n_batch, hp, nq),
        in_specs=[
            pl.BlockSpec((TM, 2 * LANES), lambda b, h, qi: (q_tile_offset + b * nq + qi, h)),
            pl.BlockSpec((sk, 2 * LANES), lambda b, h, qi: (b, h)),
            pl.BlockSpec((sk, 2 * V_HEAD), lambda b, h, qi: (b, h)),
        ],
        out_specs=pl.BlockSpec((TM, 2 * V_HEAD), lambda b, h, qi: (b * nq + qi, h)),
        out_shape=jax.ShapeDtypeStruct((n_batch * sq, MLA_HEADS * V_HEAD), BF16),
        compiler_params=_cparams(3),
        name="attention",
    )(q, k, v)


def _kv_expand(ckv, kpe, wk, wv, kn, cos_t, sin_t, table_block):
    n_rows = ckv.shape[0]
    nt = n_rows // TM
    return pl.pallas_call(
        _kv_kernel,
        grid=(nt,),
        in_specs=[
            pl.BlockSpec((TM, ckv.shape[1]), lambda i: (i, 0)),
            pl.BlockSpec((TM, LANES), lambda i: (i, 0)),
            pl.BlockSpec(wk.shape, lambda i: (0, 0)),
            pl.BlockSpec(wv.shape, lambda i: (0, 0)),
            pl.BlockSpec((1, LANES), lambda i: (0, 0)),
            pl.BlockSpec((TM, LANES), lambda i: (table_block(i), 0)),
            pl.BlockSpec((TM, LANES), lambda i: (table_block(i), 0)),
        ],
        out_specs=[pl.BlockSpec((TM, MLA_HEADS * LANES), lambda i: (i, 0)),
                   pl.BlockSpec((TM, MLA_HEADS * V_HEAD), lambda i: (i, 0))],
        out_shape=[jax.ShapeDtypeStruct((n_rows, MLA_HEADS * LANES), BF16),
                   jax.ShapeDtypeStruct((n_rows, MLA_HEADS * V_HEAD), BF16)],
        compiler_params=_cparams(1),
        name="kv_expand",
    )(ckv, kpe, wk, wv, kn, cos_t, sin_t)


def _attn_layer(h, mod, cond_of_tile, g, w, cache_ckv, cache_kpe, cos_t, sin_t, dims):
    n_tok, d = h.shape
    nt = n_tok // TM
    n_prompt, seq, n_dec, dec_seq, past = dims
    npt = n_prompt * seq // TM
    tps = dec_seq // TM
    q_lora = w["q_a_w"].shape[1]
    kv_lora = w["kv_w"].shape[1]

    qlat, ckv, kpe = pl.pallas_call(
        _attn_latent_kernel,
        grid_spec=pltpu.PrefetchScalarGridSpec(
            num_scalar_prefetch=1, grid=(nt,),
            in_specs=[
                pl.BlockSpec((TM, d), lambda i, cond: (i, 0)),
                pl.BlockSpec((None, 1, 6 * d), lambda i, cond: (cond[i], 0, 0)),
                pl.BlockSpec((1, d), lambda i, cond: (0, 0)),
                pl.BlockSpec((d, q_lora), lambda i, cond: (0, 0)),
                pl.BlockSpec((1, q_lora), lambda i, cond: (0, 0)),
                pl.BlockSpec((d, kv_lora), lambda i, cond: (0, 0)),
                pl.BlockSpec((1, kv_lora), lambda i, cond: (0, 0)),
                pl.BlockSpec((d, LANES), lambda i, cond: (0, 0)),
            ],
            out_specs=[pl.BlockSpec((TM, q_lora), lambda i, cond: (i, 0)),
                       pl.BlockSpec((TM, kv_lora), lambda i, cond: (i, 0)),
                       pl.BlockSpec((TM, LANES), lambda i, cond: (i, 0))]),
        out_shape=[jax.ShapeDtypeStruct((n_tok, q_lora), BF16),
                   jax.ShapeDtypeStruct((n_tok, kv_lora), F32),
                   jax.ShapeDtypeStruct((n_tok, LANES), F32)],
        compiler_params=_cparams(1),
        name="attn_latent",
    )(cond_of_tile, h, mod, g.reshape(1, d), w["q_a_w"], w["q_a_norm"], w["kv_w"], w["kv_norm"], w["kpe_w"])

    q_table = lambda i: jnp.where(i < npt, 0, 1 + (i - npt) % tps)
    q = pl.pallas_call(
        _q_kernel,
        grid=(nt,),
        in_specs=[
            pl.BlockSpec((TM, q_lora), lambda i: (i, 0)),
            pl.BlockSpec(w["q_b_w"].shape, lambda i: (0, 0)),
            pl.BlockSpec((1, LANES), lambda i: (0, 0)),
            pl.BlockSpec((TM, LANES), lambda i: (q_table(i), 0)),
            pl.BlockSpec((TM, LANES), lambda i: (q_table(i), 0)),
        ],
        out_specs=pl.BlockSpec((TM, MLA_HEADS * LANES), lambda i: (i, 0)),
        out_shape=jax.ShapeDtypeStruct((n_tok, MLA_HEADS * LANES), BF16),
        compiler_params=_cparams(1),
        name="q_expand",
    )(qlat, w["q_b_w"], w["q_norm"], cos_t, sin_t)

    n_p_rows = n_prompt * seq
    ckv_p, kpe_p = ckv[:n_p_rows], kpe[:n_p_rows]
    k_p, v_p = _kv_expand(ckv_p, kpe_p, w["k_w"], w["v_w"], w["k_norm"], cos_t, sin_t, lambda i: 0)
    ckv_s = jnp.concatenate([cache_ckv, ckv[n_p_rows:].reshape(n_dec, dec_seq, kv_lora)], axis=1)
    kpe_s = jnp.concatenate([cache_kpe, kpe[n_p_rows:].reshape(n_dec, dec_seq, LANES)], axis=1)
    sk = past + dec_seq
    tpk = sk // TM
    pc = past // TM
    k_s, v_s = _kv_expand(ckv_s.reshape(n_dec * sk, kv_lora), kpe_s.reshape(n_dec * sk, LANES),
                          w["k_w"], w["v_w"], w["k_norm"], cos_t, sin_t,
                          lambda i: jnp.where(i % tpk < pc, 0, 1 + i % tpk - pc))

    o_p = _attention(q, k_p, v_p, n_prompt, seq, seq, 0)
    o_s = _attention(q, k_s, v_s, n_dec, dec_seq, sk, npt)

    h_new = pl.pallas_call(
        functools.partial(_attn_out_kernel, n_prompt_tiles=npt),
        grid_spec=pltpu.PrefetchScalarGridSpec(
            num_scalar_prefetch=1, grid=(nt,),
            in_specs=[
                pl.BlockSpec((TM, d), lambda i, cond: (jnp.minimum(i, npt - 1), 0)),
                pl.BlockSpec((TM, d), lambda i, cond: (jnp.maximum(i - npt, 0), 0)),
                pl.BlockSpec((d, d), lambda i, cond: (0, 0)),
                pl.BlockSpec((TM, d), lambda i, cond: (i, 0)),
                pl.BlockSpec((None, 1, 6 * d), lambda i, cond: (cond[i], 0, 0)),
            ],
            out_specs=pl.BlockSpec((TM, d), lambda i, cond: (i, 0))),
        out_shape=jax.ShapeDtypeStruct((n_tok, d), F32),
        compiler_params=_cparams(1),
        name="attn_out",
    )(cond_of_tile, o_p, o_s, w["o_w"], h, mod)
    return h_new, ckv_p, kpe_p[:, QK_NOPE:QK_NOPE + QK_ROPE]


def _topk_rows(s, payload, k):
    n = s.shape[0]
    row = lax.broadcasted_iota(jnp.int32, s.shape, 0).astype(F32)
    vals, pays = [], []
    for _ in range(k):
        m = jnp.max(s, axis=0, keepdims=True)
        pos = jnp.min(jnp.where(s == m, row, float(n)), axis=0, keepdims=True)
        hit = row == pos
        if payload is None:
            pays.append(pos)
        else:
            pays.append(jnp.max(jnp.where(hit, payload, -1.0), axis=0, keepdims=True))
        vals.append(m)
        s = jnp.where(hit, -jnp.inf, s)
    runner_up = jnp.max(s, axis=0, keepdims=True)
    return jnp.concatenate(vals, axis=0), jnp.concatenate(pays, axis=0), runner_up


def _next_up(x):
    xi = lax.bitcast_convert_type(x, jnp.int32)
    up = lax.bitcast_convert_type(jnp.where(x > 0, xi + 1, xi - 1), F32)
    return jnp.where(x == 0, F32_MIN_NORMAL, up)


def _peer_route_kernel(cond_ref, h_ref, m_ref, g_ref, wqt_ref, keys_ref,
                       x_ref, s_ref, e_ref, th_ref, nx_ref, ex_ref, gx_ref,
                       qt_ref, sv_ref, si_ref, tie_ref):
    del cond_ref
    m = m_ref[...]
    n = _modulate(h_ref[...], g_ref[...], _mod_slice(m, 3), _mod_slice(m, 4)).astype(BF16)
    x_ref[...] = n
    qt_ref[...] = lax.dot_general(wqt_ref[...], n, (((1,), (1,)), ((), ())), preferred_element_type=F32)

    def sub_key(hp, carry):
        q = qt_ref[pl.ds(pl.multiple_of(hp * N_KEYS, N_KEYS), N_KEYS), :].astype(BF16)
        s = jnp.dot(keys_ref[hp % 2], q, preferred_element_type=F32)
        vals, pos, runner_up = _topk_rows(s, None, PEER_TOPK)
        sv_ref[hp] = vals
        si_ref[hp] = pos
        tie_ref[pl.ds(hp, 1), :] = jnp.where(runner_up == vals[PEER_TOPK - 1:PEER_TOPK], 1.0, 0.0)
        s_ref[hp] = s
        e_ref[hp] = jnp.exp(s - vals[0:1])
        return carry

    lax.fori_loop(0, 2 * PEER_HEADS, sub_key, 0)

    def head(hd, carry):
        sv0, sv1 = sv_ref[2 * hd], sv_ref[2 * hd + 1]
        si0, si1 = si_ref[2 * hd], si_ref[2 * hd + 1]
        cand = jnp.concatenate([sv0[a:a + 1] + sv1 for a in range(PEER_TOPK)], axis=0)
        cidx = jnp.concatenate([si0[a:a + 1] * float(N_KEYS) + si1 for a in range(PEER_TOPK)], axis=0)
        top_s, top_e, runner_up = _topk_rows(cand, cidx, PEER_TOPK)
        p = jnp.exp(top_s - top_s[0:1])
        z = jnp.sum(p, axis=0, keepdims=True)
        e_ref[2 * hd + 1] = e_ref[2 * hd + 1] / z
        theta = top_s[PEER_TOPK - 1:PEER_TOPK]
        tie = ((runner_up == theta) | (tie_ref[pl.ds(2 * hd, 1), :] > 0.0)
               | (tie_ref[pl.ds(2 * hd + 1, 1), :] > 0.0))
        th_ref[pl.ds(hd, 1), :] = jnp.where(tie, _next_up(theta), theta)
        explicit = tie & (top_s == theta)
        nx_ref[pl.ds(hd, 1), :] = jnp.sum(jnp.where(explicit, 1.0, 0.0), axis=0, keepdims=True)
        rows = pl.ds(pl.multiple_of(hd * PEER_TOPK, PEER_TOPK), PEER_TOPK)
        gx_ref[rows, :] = jnp.where(explicit, p / z, 0.0)
        ex_ref[rows, :] = top_e
        return carry

    lax.fori_loop(0, PEER_HEADS, head, 0)


def _peer_route(h, mod, cond_of_tile, g, wqt, keys):
    n_tok, d = h.shape
    nt = n_tok // TM
    nq = wqt.shape[0]
    return pl.pallas_call(
        _peer_route_kernel,
        grid_spec=pltpu.PrefetchScalarGridSpec(
            num_scalar_prefetch=1, grid=(nt,),
            in_specs=[
                pl.BlockSpec((TM, d), lambda i, cond: (i, 0)),
                pl.BlockSpec((None, 1, 6 * d), lambda i, cond: (cond[i], 0, 0)),
                pl.BlockSpec((1, d), lambda i, cond: (0, 0)),
                pl.BlockSpec((nq, d), lambda i, cond: (0, 0)),
                pl.BlockSpec(keys.shape, lambda i, cond: (0, 0, 0)),
            ],
            out_specs=[pl.BlockSpec((TM, d), lambda i, cond: (i, 0)),
                       pl.BlockSpec((2 * PEER_HEADS, N_KEYS, TM), lambda i, cond: (0, 0, i)),
                       pl.BlockSpec((2 * PEER_HEADS, N_KEYS, TM), lambda i, cond: (0, 0, i)),
                       pl.BlockSpec((PEER_HEADS, TM), lambda i, cond: (0, i)),
                       pl.BlockSpec((PEER_HEADS, TM), lambda i, cond: (0, i)),
                       pl.BlockSpec((ROWS_PER_TOKEN, TM), lambda i, cond: (0, i)),
                       pl.BlockSpec((ROWS_PER_TOKEN, TM), lambda i, cond: (0, i))],
            scratch_shapes=[
                pltpu.VMEM((nq, TM), F32),
                pltpu.VMEM((2 * PEER_HEADS, PEER_TOPK, TM), F32),
                pltpu.VMEM((2 * PEER_HEADS, PEER_TOPK, TM), F32),
                pltpu.VMEM((2 * PEER_HEADS, TM), F32),
            ]),
        out_shape=[jax.ShapeDtypeStruct((n_tok, d), BF16),
                   jax.ShapeDtypeStruct((2 * PEER_HEADS, N_KEYS, n_tok), F32),
                   jax.ShapeDtypeStruct((2 * PEER_HEADS, N_KEYS, n_tok), F32),
                   jax.ShapeDtypeStruct((PEER_HEADS, n_tok), F32),
                   jax.ShapeDtypeStruct((PEER_HEADS, n_tok), F32),
                   jax.ShapeDtypeStruct((ROWS_PER_TOKEN, n_tok), F32),
                   jax.ShapeDtypeStruct((ROWS_PER_TOKEN, n_tok), F32)],
        compiler_params=_cparams(1),
        name="peer_route",
    )(cond_of_tile, h, mod, g.reshape(1, d), wqt, keys)


def _peer_dense_kernel(cond_ref, nloop_ref, x_ref, s_ref, e_ref, th_ref, ex_ref, gx_ref, u_ref, vt_ref,
                       h_ref, m_ref, o_ref, st_ref, a_ref, acc_ref):
    del cond_ref
    i = pl.program_id(0)
    j = pl.program_id(1)
    n_a = EB // N_KEYS
    n_extra = nloop_ref[i]
    key = lax.broadcasted_iota(jnp.int32, (N_KEYS, LANES), 0).astype(F32)
    sub = lax.broadcasted_iota(jnp.int32, (SUBLANES, LANES), 0)

    @pl.when(j == 0)
    def _():
        acc_ref[...] = jnp.zeros_like(acc_ref)

    st_ref[...] = lax.dot_general(u_ref[...], x_ref[...], (((1,), (1,)), ((), ())), preferred_element_type=F32)

    assert n_a == SUBLANES
    a_rows = pl.ds(pl.multiple_of(j * n_a, n_a), n_a)

    def lane_chunk(lc, carry):
        lanes = pl.ds(pl.multiple_of(lc * LANES, LANES), LANES)
        for al in range(n_a):
            rows = slice(al * N_KEYS, (al + 1) * N_KEYS)
            g = jnp.zeros((N_KEYS, LANES), F32)
            for hd in range(PEER_HEADS):
                s0 = s_ref[2 * hd, a_rows, lanes][al:al + 1]
                e0 = e_ref[2 * hd, a_rows, lanes][al:al + 1]
                cand = s0 + s_ref[2 * hd + 1, :, lanes]
                wgt = e0 * e_ref[2 * hd + 1, :, lanes]
                g = g + jnp.where(cand >= th_ref[pl.ds(hd, 1), lanes], wgt, 0.0)

            e_base = ((j * n_a + al) * N_KEYS).astype(F32)

            def tied(r, g):
                slot = PEER_TOPK - 1 - r
                pick = sub == (slot % SUBLANES)
                for hd in range(PEER_HEADS):
                    grp = pl.ds(pl.multiple_of(hd * PEER_TOPK + (slot // SUBLANES) * SUBLANES, SUBLANES), SUBLANES)
                    b = jnp.sum(jnp.where(pick, ex_ref[grp, lanes], 0.0), axis=0, keepdims=True) - e_base
                    gate = jnp.sum(jnp.where(pick, gx_ref[grp, lanes], 0.0), axis=0, keepdims=True)
                    g = g + jnp.where(key == b, gate, 0.0)
                return g

            g = lax.fori_loop(0, n_extra, tied, g)
            a_ref[rows, lanes] = (g * _gelu(st_ref[rows, lanes])).astype(BF16)
        return carry

    lax.fori_loop(0, TD // LANES, lane_chunk, 0)
    acc_ref[...] += jnp.dot(vt_ref[...], a_ref[...], preferred_element_type=F32)

    @pl.when(j == pl.num_programs(1) - 1)
    def _():
        o_ref[...] = h_ref[...] + _mod_slice(m_ref[...], 5) * acc_ref[...].T


def _peer_dense(h, x, s, e, th, nloop, ex, gx, mod, cond_of_tile, u_bf, vt_bf, layer):
    n_tok, d = h.shape
    n_exp = u_bf.shape[1]
    return pl.pallas_call(
        _peer_dense_kernel,
        grid_spec=pltpu.PrefetchScalarGridSpec(
            num_scalar_prefetch=2, grid=(n_tok // TD, n_exp // EB),
            in_specs=[
                pl.BlockSpec((TD, d), lambda i, j, cond, nl: (i, 0)),
                pl.BlockSpec((2 * PEER_HEADS, N_KEYS, TD), lambda i, j, cond, nl: (0, 0, i)),
                pl.BlockSpec((2 * PEER_HEADS, N_KEYS, TD), lambda i, j, cond, nl: (0, 0, i)),
                pl.BlockSpec((PEER_HEADS, TD), lambda i, j, cond, nl: (0, i)),
                pl.BlockSpec((ROWS_PER_TOKEN, TD), lambda i, j, cond, nl: (0, i)),
                pl.BlockSpec((ROWS_PER_TOKEN, TD), lambda i, j, cond, nl: (0, i)),
                pl.BlockSpec((None, EB, d), lambda i, j, cond, nl: (layer, j, 0)),
                pl.BlockSpec((None, d, EB), lambda i, j, cond, nl: (layer, 0, j)),
                pl.BlockSpec((TD, d), lambda i, j, cond, nl: (i, 0)),
                pl.BlockSpec((None, 1, 6 * d), lambda i, j, cond, nl: (cond[i * (TD // TM)], 0, 0)),
            ],
            out_specs=pl.BlockSpec((TD, d), lambda i, j, cond, nl: (i, 0)),
            scratch_shapes=[
                pltpu.VMEM((EB, TD), F32),
                pltpu.VMEM((EB, TD), BF16),
                pltpu.VMEM((d, TD), F32),
            ]),
        out_shape=jax.ShapeDtypeStruct((n_tok, d), F32),
        compiler_params=pltpu.CompilerParams(
            dimension_semantics=("arbitrary", "arbitrary"), vmem_limit_bytes=DENSE_VMEM_LIMIT),
        name="peer_dense",
    )(cond_of_tile, nloop, x, s, e, th, ex, gx, u_bf, vt_bf, h, mod)


def _peer_apply_kernel(cond_ref, idx_hbm, gate_ref, n_ref, h_ref, m_ref, tab_all_hbm, o_ref,
                       idx_smem, rows, idx_sem, row_sem, *, layer):
    del cond_ref
    tab_hbm = tab_all_hbm.at[layer]
    i = pl.program_id(0)
    n_steps = pl.num_programs(0)
    n_rows = TB * ROWS_PER_TOKEN
    d = D_MODEL

    def idx_copy(step, slot):
        return pltpu.make_async_copy(idx_hbm.at[step], idx_smem.at[pl.ds(slot * n_rows, n_rows)], idx_sem.at[slot])

    def issue_rows(slot):
        def body(j, carry):
            base = slot * n_rows + j * SUBLANES
            for k in range(SUBLANES):
                e = idx_smem[base + k]
                pltpu.make_async_copy(tab_hbm.at[e], rows.at[slot, j, :, k, :], row_sem.at[slot]).start()
            return carry
        lax.fori_loop(0, n_rows // SUBLANES, body, 0)

    slot = i % 2
    nslot = 1 - slot

    @pl.when(i == 0)
    def _():
        idx_copy(0, 0).start()
        idx_copy(0, 0).wait()
        issue_rows(0)

        @pl.when(n_steps > 1)
        def _():
            idx_copy(1, 1).start()

    @pl.when(i + 1 < n_steps)
    def _():
        idx_copy(i + 1, nslot).wait()
        issue_rows(nslot)

        @pl.when(i + 2 < n_steps)
        def _():
            idx_copy(i + 2, slot).start()

    pltpu.make_async_copy(rows.at[nslot], rows.at[slot], row_sem.at[slot]).wait()

    n_chunks = d // LANES
    groups = ROWS_PER_TOKEN // SUBLANES
    lane = lax.broadcasted_iota(jnp.int32, (ROWS_PER_TOKEN, LANES), 1)

    def chunk(t, c):
        return rows[slot, t * groups:(t + 1) * groups, c].reshape(ROWS_PER_TOKEN, LANES)

    scores = jnp.zeros((ROWS_PER_TOKEN, LANES), F32)
    for t in range(TB):
        acc = jnp.zeros((ROWS_PER_TOKEN, LANES), F32)
        for c in range(n_chunks):
            acc = acc + chunk(t, c) * n_ref[t:t + 1, c * LANES:(c + 1) * LANES]
        scores = jnp.where(lane == t, jnp.sum(acc, axis=-1, keepdims=True), scores)
    act = _gelu(scores.T[0:TB, :])
    wgt = gate_ref[...] * act
    wgt_t = jnp.concatenate([wgt, jnp.zeros((LANES - TB, ROWS_PER_TOKEN), F32)], axis=0).T
    for t in range(TB):
        w_col = jnp.broadcast_to(wgt_t[:, t:t + 1], (ROWS_PER_TOKEN, LANES))
        for c in range(n_chunks):
            cols = slice(c * LANES, (c + 1) * LANES)
            o = jnp.sum(w_col * chunk(t, n_chunks + c), axis=0, keepdims=True)
            o_ref[t:t + 1, cols] = h_ref[t:t + 1, cols] + m_ref[:, 5 * d + c * LANES:5 * d + (c + 1) * LANES] * o


def _peer_apply(h, n, idx, gate, mod, cond_of_step, table, layer):
    n_tok, d = h.shape
    n_steps = n_tok // TB
    idx_steps = idx.reshape(n_steps, TB * ROWS_PER_TOKEN)
    return pl.pallas_call(
        functools.partial(_peer_apply_kernel, layer=layer),
        grid_spec=pltpu.PrefetchScalarGridSpec(
            num_scalar_prefetch=1, grid=(n_steps,),
            in_specs=[
                pl.BlockSpec(memory_space=pl.ANY),
                pl.BlockSpec((TB, ROWS_PER_TOKEN), lambda i, cond: (i, 0)),
                pl.BlockSpec((TB, d), lambda i, cond: (i, 0)),
                pl.BlockSpec((TB, d), lambda i, cond: (i, 0)),
                pl.BlockSpec((None, 1, 6 * d), lambda i, cond: (cond[i], 0, 0)),
                pl.BlockSpec(memory_space=pl.ANY),
            ],
            out_specs=pl.BlockSpec((TB, d), lambda i, cond: (i, 0)),
            scratch_shapes=[
                pltpu.SMEM((2 * TB * ROWS_PER_TOKEN,), jnp.int32),
                pltpu.VMEM((2, TB * ROWS_PER_TOKEN // SUBLANES, 2 * d // LANES, SUBLANES, LANES), F32),
                pltpu.SemaphoreType.DMA((2,)),
                pltpu.SemaphoreType.DMA((2,)),
            ]),
        out_shape=jax.ShapeDtypeStruct((n_tok, d), F32),
        compiler_params=_cparams(1),
        name="peer_apply",
    )(cond_of_step, idx_steps, gate, n, h, mod, table)


def _rope_tables(seq_len):
    rows = seq_len // GRID_W
    row = jnp.repeat(jnp.arange(rows), GRID_W).astype(F32)
    col = jnp.tile(jnp.arange(GRID_W), rows).astype(F32)
    half = QK_ROPE // 2
    inv = ROPE_THETA ** (-(jnp.arange(half // 2, dtype=F32) * 2.0 / half))
    ang = jnp.concatenate([row[:, None] * inv, col[:, None] * inv], axis=-1)
    cos, sin = jnp.cos(ang), jnp.sin(ang)
    cos_pairs = jnp.repeat(cos, 2, axis=-1)
    sin_pairs = jnp.stack([-sin, sin], axis=-1).reshape(seq_len, QK_ROPE)
    cos_t = jnp.ones((TM + seq_len, LANES), F32).at[TM:, QK_NOPE:QK_DIM].set(cos_pairs)
    sin_t = jnp.zeros((TM + seq_len, LANES), F32).at[TM:, QK_NOPE:QK_DIM].set(sin_pairs)
    return cos_t, sin_t


def _pad_heads(w, width):
    k = w.shape[0]
    w3 = w.reshape(k, MLA_HEADS, width)
    return jnp.pad(w3, ((0, 0), (0, 0), (0, LANES - width))).reshape(k, MLA_HEADS * LANES)


def _attn_weights(j, q_a_w, q_a_norm, q_b_w, kv_a_w, kv_a_norm, kv_b_w, q_norm, k_norm, o_w):
    d = q_a_w.shape[1]
    kv_lora = kv_a_norm.shape[1]
    kvb = kv_b_w[j].reshape(kv_lora, MLA_HEADS, QK_NOPE + V_HEAD)
    lane_pad = (0, LANES - QK_DIM)
    return {
        "q_a_w": q_a_w[j].astype(BF16),
        "q_a_norm": q_a_norm[j].reshape(1, -1),
        "q_b_w": _pad_heads(q_b_w[j], QK_DIM).astype(BF16),
        "kv_w": kv_a_w[j][:, :kv_lora].astype(BF16),
        "kv_norm": kv_a_norm[j].reshape(1, -1),
        "kpe_w": jnp.pad(kv_a_w[j][:, kv_lora:], ((0, 0), (QK_NOPE, LANES - QK_DIM))).astype(BF16),
        "k_w": _pad_heads(kvb[:, :, :QK_NOPE].reshape(kv_lora, MLA_HEADS * QK_NOPE), QK_NOPE).astype(BF16),
        "v_w": kvb[:, :, QK_NOPE:].reshape(kv_lora, MLA_HEADS * V_HEAD).astype(BF16),
        "q_norm": jnp.pad(q_norm[j], lane_pad).reshape(1, LANES),
        "k_norm": jnp.pad(k_norm[j], lane_pad).reshape(1, LANES),
        "o_w": o_w[j].astype(BF16),
    }


def kernel(x_prompt, x_sample, cache_ckv, cache_kpe, c, c_ctx, mod_w, mod_b, norm1_g, norm2_g, conv_w_in, conv_b_in, conv_dw, conv_dw_b, conv_norm_g, conv_w_out, conv_b_out, q_a_w, q_a_norm, q_b_w, kv_a_w, kv_a_norm, kv_b_w, q_norm, k_norm, o_w, peer_wq, peer_keys, peer_u, peer_v):
    n_prompt, seq, d = x_prompt.shape
    n_dec, dec_seq, _ = x_sample.shape
    past = cache_ckv.shape[2]
    depth = mod_w.shape[0]
    assert d == D_MODEL and seq % TM == 0 and dec_seq % TM == 0 and past % TM == 0
    assert 1 + n_dec <= 8
    assert (n_prompt * seq) % TD == 0 and dec_seq % TD == 0 and peer_u.shape[1] % EB == 0
    dims = (n_prompt, seq, n_dec, dec_seq, past)

    n_p_rows = n_prompt * seq
    h = jnp.concatenate([x_prompt.reshape(n_p_rows, d), x_sample.reshape(n_dec * dec_seq, d)], axis=0)
    n_tok = h.shape[0]

    tiles_p, tps = n_p_rows // TM, dec_seq // TM
    tile = jnp.arange(n_tok // TM, dtype=jnp.int32)
    in_p = tile < tiles_p
    cond_of_tile = jnp.where(in_p, 0, 1 + (tile - tiles_p) // tps).astype(jnp.int32)
    pos_p, pos_s = tile % (seq // TM), (tile - tiles_p) % tps
    seq_first = jnp.where(in_p, pos_p == 0, pos_s == 0).astype(jnp.int32)
    seq_last = jnp.where(in_p, pos_p == seq // TM - 1, pos_s == tps - 1).astype(jnp.int32)
    cond_of_step = jnp.repeat(cond_of_tile, TM // TB)

    cond8 = jnp.zeros((8, d), F32).at[0].set(c_ctx).at[1:1 + n_dec].set(c)
    mod_all = _modulation_all(cond8, mod_w, mod_b)
    u_bf = peer_u.astype(BF16)
    vt_bf = jnp.swapaxes(peer_v.astype(BF16), 1, 2)
    cos_t, sin_t = _rope_tables(dec_seq)
    cache_kpe_pad = jnp.pad(cache_kpe, ((0, 0), (0, 0), (0, 0), (QK_NOPE, LANES - QK_DIM)))

    new_ckv, new_kpe = [], []
    for i in range(depth):
        mod = mod_all[i].reshape(8, 1, 6 * d)
        j = i // 2
        if i % 2 == 0:
            h = _conv_layer(h, mod, cond_of_tile, seq_first, seq_last, norm1_g[i],
                            conv_w_in[j].astype(BF16), conv_b_in[j], conv_dw[j], conv_dw_b[j],
                            conv_norm_g[j], conv_w_out[j].astype(BF16), conv_b_out[j])
        else:
            w = _attn_weights(j, q_a_w, q_a_norm, q_b_w, kv_a_w, kv_a_norm, kv_b_w, q_norm, k_norm, o_w)
            h, ckv_p, kpe_p = _attn_layer(h, mod, cond_of_tile, norm1_g[i], w, cache_ckv[:, j],
                                          cache_kpe_pad[:, j], cos_t, sin_t, dims)
            new_ckv.append(ckv_p.reshape(n_prompt, seq, -1))
            new_kpe.append(kpe_p.reshape(n_prompt, seq, -1))
        x, s, e, th, nx, ex, gx = _peer_route(h, mod, cond_of_tile, norm2_g[i], peer_wq[i].T.astype(BF16),
                                              peer_keys[i].astype(BF16))
        nloop = jnp.max(nx.reshape(PEER_HEADS, n_tok // TD, TD), axis=(0, 2)).astype(jnp.int32)
        h = _peer_dense(h, x, s, e, th, nloop, ex, gx, mod, cond_of_tile, u_bf, vt_bf, i)

    y_prompt = h[:n_p_rows].reshape(n_prompt, seq, d)
    y_sample = h[n_p_rows:].reshape(n_dec, dec_seq, d)
    return (y_prompt, y_sample, jnp.stack(new_ckv, axis=1), jnp.stack(new_kpe, axis=1))
```

```python
import functools

import jax
import jax.numpy as jnp
from jax import lax
from jax.experimental import pallas as pl
from jax.experimental.pallas import tpu as pltpu

F32 = jnp.float32
BF16 = jnp.bfloat16

D_MODEL = 1024
GRID_W = 64
CONV_K = 31
MLA_HEADS = 16
QK_NOPE = 64
QK_ROPE = 32
V_HEAD = 64
QK_DIM = QK_NOPE + QK_ROPE
ROPE_THETA = 10000.0
PEER_HEADS = 8
N_KEYS = 128
PEER_TOPK = 16
EPS = 1e-6

LANES = 128
SUBLANES = 8
TM = 256
HALO = 16
TD = 512
TC = 512
EB = 1024
DENSE_VMEM_LIMIT = 56 * 1024 * 1024
F32_MIN_NORMAL = 1.1754943508222875e-38
ROWS_PER_TOKEN = PEER_HEADS * PEER_TOPK
VMEM_LIMIT = 48 * 1024 * 1024


def _cparams(n_grid):
    return pltpu.CompilerParams(
        dimension_semantics=("arbitrary",) * n_grid, vmem_limit_bytes=VMEM_LIMIT)


def _rms(x, g, denom=None):
    n = x.shape[-1] if denom is None else denom
    ms = jnp.sum(x * x, axis=-1, keepdims=True) / float(n)
    return x * lax.rsqrt(ms + EPS) * g


def _modulate(h, g, shift, scale):
    return _rms(h, g) * (1.0 + scale) + shift


def _gelu(x):
    return 0.5 * x * (1.0 + lax.erf(x * (0.5 ** 0.5)))


def _mod_slice(m, k):
    return m[:, k * D_MODEL:(k + 1) * D_MODEL]


def _mod_kernel(c_ref, w_ref, b_ref, o_ref):
    c = c_ref[...]
    a = (c * jax.nn.sigmoid(c)).astype(BF16)
    o_ref[...] = jnp.dot(a, w_ref[...].astype(BF16), preferred_element_type=F32) + b_ref[...]


def _modulation_all(cond8, mod_w, mod_b):
    depth, d, n6 = mod_w.shape
    tn = 1536
    return pl.pallas_call(
        _mod_kernel,
        grid=(depth, n6 // tn),
        in_specs=[
            pl.BlockSpec((8, d), lambda l, n: (0, 0)),
            pl.BlockSpec((None, d, tn), lambda l, n: (l, 0, n)),
            pl.BlockSpec((None, 1, tn), lambda l, n: (l, 0, n)),
        ],
        out_specs=pl.BlockSpec((None, 8, tn), lambda l, n: (l, 0, n)),
        out_shape=jax.ShapeDtypeStruct((depth, 8, n6), F32),
        compiler_params=_cparams(2),
        name="modulation",
    )(cond8, mod_w, mod_b.reshape(depth, 1, n6))


def _conv_in_kernel(cond_ref, h_ref, m_ref, g_ref, w_ref, b_ref, o_ref):
    del cond_ref
    m = m_ref[...]
    n = _modulate(h_ref[...], g_ref[...], _mod_slice(m, 0), _mod_slice(m, 1))
    hh = jnp.dot(n.astype(BF16), w_ref[...], preferred_element_type=F32) + b_ref[...]
    o_ref[...] = hh[:, :D_MODEL] * jax.nn.sigmoid(hh[:, D_MODEL:])


def _conv_out_kernel(first_ref, last_ref, cond_ref, cur_ref, prev_ref, next_ref, dw_ref, dwb_ref, ng_ref,
                     w_ref, b_ref, h_ref, m_ref, o_ref, pad_ref, conv_ref):
    del cond_ref
    i = pl.program_id(0)
    keep_prev = jnp.where(first_ref[i] == 0, 1.0, 0.0)
    keep_next = jnp.where(last_ref[i] == 0, 1.0, 0.0)
    pad_ref[0:HALO, :] = prev_ref[...] * keep_prev
    pad_ref[HALO:HALO + TM, :] = cur_ref[...]
    pad_ref[HALO + TM:HALO + TM + HALO, :] = next_ref[...] * keep_next
    base = HALO - CONV_K // 2
    for c in range(D_MODEL // LANES):
        cols = slice(c * LANES, (c + 1) * LANES)
        acc = jnp.zeros((TM, LANES), F32) + dwb_ref[:, cols]
        for k in range(CONV_K):
            acc = acc + dw_ref[k:k + 1, cols] * pad_ref[base + k:base + k + TM, cols]
        conv_ref[:, cols] = acc
    y = _rms(conv_ref[...], ng_ref[...])
    y = y * jax.nn.sigmoid(y)
    out = jnp.dot(y.astype(BF16), w_ref[...], preferred_element_type=F32) + b_ref[...]
    o_ref[...] = h_ref[...] + _mod_slice(m_ref[...], 2) * out


def _conv_layer(h, mod, cond_of_tile, seq_first, seq_last, g, w_in, b_in, dw, dw_b, ng, w_out, b_out):
    n_tok, d = h.shape
    nt = n_tok // TM
    hb = TM // HALO
    tile = lambda i, *_: (i, 0)
    full = lambda i, *_: (0, 0)
    modmap = lambda i, cond, *_: (cond[i], 0, 0)
    glu = pl.pallas_call(
        _conv_in_kernel,
        grid_spec=pltpu.PrefetchScalarGridSpec(
            num_scalar_prefetch=1, grid=(nt,),
            in_specs=[
                pl.BlockSpec((TM, d), lambda i, cond: (i, 0)),
                pl.BlockSpec((None, 1, 6 * d), lambda i, cond: (cond[i], 0, 0)),
                pl.BlockSpec((1, d), lambda i, cond: (0, 0)),
                pl.BlockSpec((d, 2 * d), lambda i, cond: (0, 0)),
                pl.BlockSpec((1, 2 * d), lambda i, cond: (0, 0)),
            ],
            out_specs=pl.BlockSpec((TM, d), lambda i, cond: (i, 0))),
        out_shape=jax.ShapeDtypeStruct((n_tok, d), F32),
        compiler_params=_cparams(1),
        name="conv_in",
    )(cond_of_tile, h, mod, g.reshape(1, d), w_in, b_in.reshape(1, 2 * d))

    n_hb = n_tok // HALO
    dw_pad = jnp.zeros((32, d), F32).at[:CONV_K].set(dw)
    return pl.pallas_call(
        _conv_out_kernel,
        grid_spec=pltpu.PrefetchScalarGridSpec(
            num_scalar_prefetch=3, grid=(nt,),
            in_specs=[
                pl.BlockSpec((TM, d), lambda i, f, l, cond: (i, 0)),
                pl.BlockSpec((HALO, d), lambda i, f, l, cond: (jnp.maximum(i * hb - 1, 0), 0)),
                pl.BlockSpec((HALO, d), lambda i, f, l, cond: (jnp.minimum((i + 1) * hb, n_hb - 1), 0)),
                pl.BlockSpec((32, d), lambda i, f, l, cond: (0, 0)),
                pl.BlockSpec((1, d), lambda i, f, l, cond: (0, 0)),
                pl.BlockSpec((1, d), lambda i, f, l, cond: (0, 0)),
                pl.BlockSpec((d, d), lambda i, f, l, cond: (0, 0)),
                pl.BlockSpec((1, d), lambda i, f, l, cond: (0, 0)),
                pl.BlockSpec((TM, d), lambda i, f, l, cond: (i, 0)),
                pl.BlockSpec((None, 1, 6 * d), lambda i, f, l, cond: (cond[i], 0, 0)),
            ],
            out_specs=pl.BlockSpec((TM, d), lambda i, f, l, cond: (i, 0)),
            scratch_shapes=[pltpu.VMEM((TM + 2 * HALO, d), F32), pltpu.VMEM((TM, d), F32)]),
        out_shape=jax.ShapeDtypeStruct((n_tok, d), F32),
        compiler_params=_cparams(1),
        name="conv_out",
    )(seq_first, seq_last, cond_of_tile, glu, glu, glu, dw_pad, dw_b.reshape(1, d), ng.reshape(1, d),
      w_out, b_out.reshape(1, d), h, mod)


def _rope(y, cos, sin):
    lane = lax.broadcasted_iota(jnp.int32, y.shape, 1)
    nxt = pltpu.roll(y, LANES - 1, 1)
    prv = pltpu.roll(y, 1, 1)
    partner = jnp.where((lane & 1) == 0, nxt, prv)
    return y * cos + partner * sin


def _attn_latent_kernel(cond_ref, h_ref, m_ref, g_ref, qaw_ref, qan_ref, kvw_ref, kvn_ref, kpew_ref,
                        qlat_ref, ckv_ref, kpe_ref):
    del cond_ref
    m = m_ref[...]
    n = _modulate(h_ref[...], g_ref[...], _mod_slice(m, 0), _mod_slice(m, 1)).astype(BF16)
    qa = jnp.dot(n, qaw_ref[...], preferred_element_type=F32)
    qlat_ref[...] = _rms(qa, qan_ref[...]).astype(BF16)
    kv = jnp.dot(n, kvw_ref[...], preferred_element_type=F32)
    ckv_ref[...] = _rms(kv, kvn_ref[...])
    kpe_ref[...] = jnp.dot(n, kpew_ref[...], preferred_element_type=F32)


def _q_kernel(qlat_ref, w_ref, qn_ref, cos_ref, sin_ref, o_ref):
    q = jnp.dot(qlat_ref[...], w_ref[...], preferred_element_type=F32)
    cos = cos_ref[...]
    sin = sin_ref[...]
    qn = qn_ref[...]
    for hd in range(MLA_HEADS):
        cols = slice(hd * LANES, (hd + 1) * LANES)
        y = _rms(q[:, cols], qn, denom=QK_DIM)
        o_ref[:, cols] = _rope(y, cos, sin).astype(BF16)


def _kv_kernel(ckv_ref, kpe_ref, wk_ref, wv_ref, kn_ref, cos_ref, sin_ref, k_ref, v_ref):
    c = ckv_ref[...].astype(BF16)
    kf = jnp.dot(c, wk_ref[...], preferred_element_type=F32)
    v_ref[...] = jnp.dot(c, wv_ref[...], preferred_element_type=F32).astype(BF16)
    kpe = kpe_ref[...]
    cos = cos_ref[...]
    sin = sin_ref[...]
    kn = kn_ref[...]
    for hd in range(MLA_HEADS):
        cols = slice(hd * LANES, (hd + 1) * LANES)
        y = _rms(kf[:, cols] + kpe, kn, denom=QK_DIM)
        k_ref[:, cols] = _rope(y, cos, sin).astype(BF16)


def _attn_kernel(q_ref, k_ref, v_ref, o_ref):
    scale = float(QK_DIM) ** -0.5
    for hh in range(2):
        q = q_ref[:, hh * LANES:(hh + 1) * LANES]
        k = k_ref[:, hh * LANES:(hh + 1) * LANES]
        s = lax.dot_general(q, k, (((1,), (1,)), ((), ())), preferred_element_type=F32) * scale
        p = jnp.exp(s - jnp.max(s, axis=-1, keepdims=True))
        l = jnp.sum(p, axis=-1, keepdims=True)
        o = jnp.dot(p.astype(BF16), v_ref[:, hh * V_HEAD:(hh + 1) * V_HEAD], preferred_element_type=F32)
        o_ref[:, hh * V_HEAD:(hh + 1) * V_HEAD] = (o / l).astype(BF16)


def _attn_out_kernel(cond_ref, op_ref, os_ref, w_ref, h_ref, m_ref, o_ref, *, n_prompt_tiles):
    del cond_ref
    i = pl.program_id(0)
    o = jnp.where(i < n_prompt_tiles, op_ref[...], os_ref[...])
    out = jnp.dot(o, w_ref[...], preferred_element_type=F32)
    o_ref[...] = h_ref[...] + _mod_slice(m_ref[...], 2) * out


def _attention(q, k, v, n_batch, sq, sk, q_tile_offset):
    nq = sq // TM
    hp = MLA_HEADS // 2
    return pl.pallas_call(
        _attn_kernel,
        grid=(n_batch, hp, nq),
        in_specs=[
            pl.BlockSpec((TM, 2 * LANES), lambda b, h, qi: (q_tile_offset + b * nq + qi, h)),
            pl.BlockSpec((sk, 2 * LANES), lambda b, h, qi: (b, h)),
            pl.BlockSpec((sk, 2 * V_HEAD), lambda b, h, qi: (b, h)),
        ],
        out_specs=pl.BlockSpec((TM, 2 * V_HEAD), lambda b, h, qi: (b * nq + qi, h)),
        out_shape=jax.ShapeDtypeStruct((n_batch * sq, MLA_HEADS * V_HEAD), BF16),
        compiler_params=_cparams(3),
        name="attention",
    )(q, k, v)


def _kv_expand(ckv, kpe, wk, wv, kn, cos_t, sin_t, table_block):
    n_rows = ckv.shape[0]
    nt = n_rows // TM
    return pl.pallas_call(
        _kv_kernel,
        grid=(nt,),
        in_specs=[
            pl.BlockSpec((TM, ckv.shape[1]), lambda i: (i, 0)),
            pl.BlockSpec((TM, LANES), lambda i: (i, 0)),
            pl.BlockSpec(wk.shape, lambda i: (0, 0)),
            pl.BlockSpec(wv.shape, lambda i: (0, 0)),
            pl.BlockSpec((1, LANES), lambda i: (0, 0)),
            pl.BlockSpec((TM, LANES), lambda i: (table_block(i), 0)),
            pl.BlockSpec((TM, LANES), lambda i: (table_block(i), 0)),
        ],
        out_specs=[pl.BlockSpec((TM, MLA_HEADS * LANES), lambda i: (i, 0)),
                   pl.BlockSpec((TM, MLA_HEADS * V_HEAD), lambda i: (i, 0))],
        out_shape=[jax.ShapeDtypeStruct((n_rows, MLA_HEADS * LANES), BF16),
                   jax.ShapeDtypeStruct((n_rows, MLA_HEADS * V_HEAD), BF16)],
        compiler_params=_cparams(1),
        name="kv_expand",
    )(ckv, kpe, wk, wv, kn, cos_t, sin_t)


def _attn_layer(h, mod, cond_of_tile, g, w, cache_ckv, cache_kpe, cos_t, sin_t, dims):
    n_tok, d = h.shape
    nt = n_tok // TM
    n_prompt, seq, n_dec, dec_seq, past = dims
    npt = n_prompt * seq // TM
    tps = dec_seq // TM
    q_lora = w["q_a_w"].shape[1]
    kv_lora = w["kv_w"].shape[1]

    qlat, ckv, kpe = pl.pallas_call(
        _attn_latent_kernel,
        grid_spec=pltpu.PrefetchScalarGridSpec(
            num_scalar_prefetch=1, grid=(nt,),
            in_specs=[
                pl.BlockSpec((TM, d), lambda i, cond: (i, 0)),
                pl.BlockSpec((None, 1, 6 * d), lambda i, cond: (cond[i], 0, 0)),
                pl.BlockSpec((1, d), lambda i, cond: (0, 0)),
                pl.BlockSpec((d, q_lora), lambda i, cond: (0, 0)),
                pl.BlockSpec((1, q_lora), lambda i, cond: (0, 0)),
                pl.BlockSpec((d, kv_lora), lambda i, cond: (0, 0)),
                pl.BlockSpec((1, kv_lora), lambda i, cond: (0, 0)),
                pl.BlockSpec((d, LANES), lambda i, cond: (0, 0)),
            ],
            out_specs=[pl.BlockSpec((TM, q_lora), lambda i, cond: (i, 0)),
                       pl.BlockSpec((TM, kv_lora), lambda i, cond: (i, 0)),
                       pl.BlockSpec((TM, LANES), lambda i, cond: (i, 0))]),
        out_shape=[jax.ShapeDtypeStruct((n_tok, q_lora), BF16),
                   jax.ShapeDtypeStruct((n_tok, kv_lora), F32),
                   jax.ShapeDtypeStruct((n_tok, LANES), F32)],
        compiler_params=_cparams(1),
        name="attn_latent",
    )(cond_of_tile, h, mod, g.reshape(1, d), w["q_a_w"], w["q_a_norm"], w["kv_w"], w["kv_norm"], w["kpe_w"])

    q_table = lambda i: jnp.where(i < npt, 0, 1 + (i - npt) % tps)
    q = pl.pallas_call(
        _q_kernel,
        grid=(nt,),
        in_specs=[
            pl.BlockSpec((TM, q_lora), lambda i: (i, 0)),
            pl.BlockSpec(w["q_b_w"].shape, lambda i: (0, 0)),
            pl.BlockSpec((1, LANES), lambda i: (0, 0)),
            pl.BlockSpec((TM, LANES), lambda i: (q_table(i), 0)),
            pl.BlockSpec((TM, LANES), lambda i: (q_table(i), 0)),
        ],
        out_specs=pl.BlockSpec((TM, MLA_HEADS * LANES), lambda i: (i, 0)),
        out_shape=jax.ShapeDtypeStruct((n_tok, MLA_HEADS * LANES), BF16),
        compiler_params=_cparams(1),
        name="q_expand",
    )(qlat, w["q_b_w"], w["q_norm"], cos_t, sin_t)

    n_p_rows = n_prompt * seq
    ckv_p, kpe_p = ckv[:n_p_rows], kpe[:n_p_rows]
    k_p, v_p = _kv_expand(ckv_p, kpe_p, w["k_w"], w["v_w"], w["k_norm"], cos_t, sin_t, lambda i: 0)
    ckv_s = jnp.concatenate([cache_ckv, ckv[n_p_rows:].reshape(n_dec, dec_seq, kv_lora)], axis=1)
    kpe_s = jnp.concatenate([cache_kpe, kpe[n_p_rows:].reshape(n_dec, dec_seq, LANES)], axis=1)
    sk = past + dec_seq
    tpk = sk // TM
    pc = past // TM
    k_s, v_s = _kv_expand(ckv_s.reshape(n_dec * sk, kv_lora), kpe_s.reshape(n_dec * sk, LANES),
                          w["k_w"], w["v_w"], w["k_norm"], cos_t, sin_t,
                          lambda i: jnp.where(i % tpk < pc, 0, 1 + i % tpk - pc))

    o_p = _attention(q, k_p, v_p, n_prompt, seq, seq, 0)
    o_s = _attention(q, k_s, v_s, n_dec, dec_seq, sk, npt)

    h_new = pl.pallas_call(
        functools.partial(_attn_out_kernel, n_prompt_tiles=npt),
        grid_spec=pltpu.PrefetchScalarGridSpec(
            num_scalar_prefetch=1, grid=(nt,),
            in_specs=[
                pl.BlockSpec((TM, d), lambda i, cond: (jnp.minimum(i, npt - 1), 0)),
                pl.BlockSpec((TM, d), lambda i, cond: (jnp.maximum(i - npt, 0), 0)),
                pl.BlockSpec((d, d), lambda i, cond: (0, 0)),
                pl.BlockSpec((TM, d), lambda i, cond: (i, 0)),
                pl.BlockSpec((None, 1, 6 * d), lambda i, cond: (cond[i], 0, 0)),
            ],
            out_specs=pl.BlockSpec((TM, d), lambda i, cond: (i, 0))),
        out_shape=jax.ShapeDtypeStruct((n_tok, d), F32),
        compiler_params=_cparams(1),
        name="attn_out",
    )(cond_of_tile, o_p, o_s, w["o_w"], h, mod)
    return h_new, ckv_p, kpe_p[:, QK_NOPE:QK_NOPE + QK_ROPE]


def _topk_rows(s, payload, k):
    n = s.shape[0]
    row = lax.broadcasted_iota(jnp.int32, s.shape, 0).astype(F32)
    vals, pays = [], []
    for _ in range(k):
        m = jnp.max(s, axis=0, keepdims=True)
        pos = jnp.min(jnp.where(s == m, row, float(n)), axis=0, keepdims=True)
        hit = row == pos
        if payload is None:
            pays.append(pos)
        else:
            pays.append(jnp.max(jnp.where(hit, payload, -1.0), axis=0, keepdims=True))
        vals.append(m)
        s = jnp.where(hit, -jnp.inf, s)
    runner_up = jnp.max(s, axis=0, keepdims=True)
    return jnp.concatenate(vals, axis=0), jnp.concatenate(pays, axis=0), runner_up


def _next_up(x):
    xi = lax.bitcast_convert_type(x, jnp.int32)
    up = lax.bitcast_convert_type(jnp.where(x > 0, xi + 1, xi - 1), F32)
    return jnp.where(x == 0, F32_MIN_NORMAL, up)


def _peer_route_kernel(cond_ref, h_ref, m_ref, g_ref, wqt_ref, keys_ref,
                       x_ref, s_ref, e_ref, th_ref, nx_ref, ex_ref, gx_ref,
                       qt_ref, sv_ref, si_ref, tie_ref):
    del cond_ref
    m = m_ref[...]
    n = _modulate(h_ref[...], g_ref[...], _mod_slice(m, 3), _mod_slice(m, 4)).astype(BF16)
    x_ref[...] = n
    qt_ref[...] = lax.dot_general(wqt_ref[...], n, (((1,), (1,)), ((), ())), preferred_element_type=F32)

    def sub_keys(hd, carry):
        for p in range(2):
            hp = 2 * hd + p
            q = qt_ref[pl.ds(pl.multiple_of(hp * N_KEYS, N_KEYS), N_KEYS), :].astype(BF16)
            s = jnp.dot(keys_ref[p], q, preferred_element_type=F32)
            vals, pos, runner_up = _topk_rows(s, None, PEER_TOPK)
            sv_ref[hp] = vals
            si_ref[hp] = pos
            tie_ref[pl.ds(hp, 1), :] = jnp.where(runner_up == vals[PEER_TOPK - 1:PEER_TOPK], 1.0, 0.0)
            s_ref[hp] = s
            e_ref[hp] = jnp.exp(s - vals[0:1])
        return carry

    lax.fori_loop(0, PEER_HEADS, sub_keys, 0)

    def head(hd, carry):
        sv0, sv1 = sv_ref[2 * hd], sv_ref[2 * hd + 1]
        si0, si1 = si_ref[2 * hd], si_ref[2 * hd + 1]
        cand_rows, cidx_rows = [], []
        for a in range(PEER_TOPK):
            nb = min(PEER_TOPK, (PEER_TOPK + 1) // (a + 1))
            cand_rows.append(sv0[a:a + 1] + sv1[0:nb])
            cidx_rows.append(si0[a:a + 1] * float(N_KEYS) + si1[0:nb])
        n_cand = sum(r.shape[0] for r in cand_rows)
        pad = -n_cand % SUBLANES
        cand = jnp.concatenate(cand_rows + [jnp.full((pad, TM), -jnp.inf, F32)], axis=0)
        cidx = jnp.concatenate(cidx_rows + [jnp.zeros((pad, TM), F32)], axis=0)
        top_s, top_e, runner_up = _topk_rows(cand, cidx, PEER_TOPK)
        p = jnp.exp(top_s - top_s[0:1])
        z = jnp.sum(p, axis=0, keepdims=True)
        e_ref[2 * hd + 1] = e_ref[2 * hd + 1] / z
        theta = top_s[PEER_TOPK - 1:PEER_TOPK]
        tie = ((runner_up == theta) | (tie_ref[pl.ds(2 * hd, 1), :] > 0.0)
               | (tie_ref[pl.ds(2 * hd + 1, 1), :] > 0.0))
        th_ref[pl.ds(hd, 1), :] = jnp.where(tie, _next_up(theta), theta)
        explicit = tie & (top_s == theta)
        nx_ref[pl.ds(hd, 1), :] = jnp.sum(jnp.where(explicit, 1.0, 0.0), axis=0, keepdims=True)
        rows = pl.ds(pl.multiple_of(hd * PEER_TOPK, PEER_TOPK), PEER_TOPK)
        gx_ref[rows, :] = jnp.where(explicit, p / z, 0.0)
        ex_ref[rows, :] = top_e
        return carry

    lax.fori_loop(0, PEER_HEADS, head, 0)


def _peer_route(h, mod, cond_of_tile, g, wqt, keys):
    n_tok, d = h.shape
    nt = n_tok // TM
    nq = wqt.shape[0]
    return pl.pallas_call(
        _peer_route_kernel,
        grid_spec=pltpu.PrefetchScalarGridSpec(
            num_scalar_prefetch=1, grid=(nt,),
            in_specs=[
                pl.BlockSpec((TM, d), lambda i, cond: (i, 0)),
                pl.BlockSpec((None, 1, 6 * d), lambda i, cond: (cond[i], 0, 0)),
                pl.BlockSpec((1, d), lambda i, cond: (0, 0)),
                pl.BlockSpec((nq, d), lambda i, cond: (0, 0)),
                pl.BlockSpec(keys.shape, lambda i, cond: (0, 0, 0)),
            ],
            out_specs=[pl.BlockSpec((TM, d), lambda i, cond: (i, 0)),
                       pl.BlockSpec((2 * PEER_HEADS, N_KEYS, TM), lambda i, cond: (0, 0, i)),
                       pl.BlockSpec((2 * PEER_HEADS, N_KEYS, TM), lambda i, cond: (0, 0, i)),
                       pl.BlockSpec((PEER_HEADS, TM), lambda i, cond: (0, i)),
                       pl.BlockSpec((PEER_HEADS, TM), lambda i, cond: (0, i)),
                       pl.BlockSpec((ROWS_PER_TOKEN, TM), lambda i, cond: (0, i)),
                       pl.BlockSpec((ROWS_PER_TOKEN, TM), lambda i, cond: (0, i))],
            scratch_shapes=[
                pltpu.VMEM((nq, TM), F32),
                pltpu.VMEM((2 * PEER_HEADS, PEER_TOPK, TM), F32),
                pltpu.VMEM((2 * PEER_HEADS, PEER_TOPK, TM), F32),
                pltpu.VMEM((2 * PEER_HEADS, TM), F32),
            ]),
        out_shape=[jax.ShapeDtypeStruct((n_tok, d), BF16),
                   jax.ShapeDtypeStruct((2 * PEER_HEADS, N_KEYS, n_tok), F32),
                   jax.ShapeDtypeStruct((2 * PEER_HEADS, N_KEYS, n_tok), F32),
                   jax.ShapeDtypeStruct((PEER_HEADS, n_tok), F32),
                   jax.ShapeDtypeStruct((PEER_HEADS, n_tok), F32),
                   jax.ShapeDtypeStruct((ROWS_PER_TOKEN, n_tok), F32),
                   jax.ShapeDtypeStruct((ROWS_PER_TOKEN, n_tok), F32)],
        compiler_params=_cparams(1),
        name="peer_route",
    )(cond_of_tile, h, mod, g.reshape(1, d), wqt, keys)


def _peer_dense_kernel(cond_ref, nloop_ref, x_ref, s_ref, e_ref, th_ref, ex_ref, gx_ref, u_ref, vt_ref,
                       h_ref, m_ref, o_ref, st_ref, a_ref, acc_ref):
    del cond_ref
    i = pl.program_id(0)
    j = pl.program_id(1)
    n_a = EB // N_KEYS
    assert n_a == SUBLANES
    n_lc = TD // LANES
    n_extra = [nloop_ref[i * n_lc + lc] for lc in range(n_lc)]
    a_rows = pl.ds(pl.multiple_of(j * n_a, n_a), n_a)
    nt_dims = (((1,), (1,)), ((), ()))

    @pl.when(j == 0)
    def _():
        acc_ref[...] = jnp.zeros_like(acc_ref)

    for c in range(TD // TC):
        chunk = slice(c * TC, (c + 1) * TC)
        st_ref[:, chunk] = lax.dot_general(u_ref[...], x_ref[chunk, :], nt_dims, preferred_element_type=F32)
        for lc in range(c * TC // LANES, (c + 1) * TC // LANES):
            lanes = slice(lc * LANES, (lc + 1) * LANES)
            for al in range(n_a):
                rows = slice(al * N_KEYS, (al + 1) * N_KEYS)
                g = jnp.zeros((N_KEYS, LANES), F32)
                for hd in range(PEER_HEADS):
                    s0 = s_ref[2 * hd, a_rows, lanes][al:al + 1]
                    e0 = e_ref[2 * hd, a_rows, lanes][al:al + 1]
                    cand = s0 + s_ref[2 * hd + 1, :, lanes]
                    wgt = e0 * e_ref[2 * hd + 1, :, lanes]
                    g = g + jnp.where(cand >= th_ref[hd:hd + 1, lanes], wgt, 0.0)
                a_ref[rows, lanes] = (g * _gelu(st_ref[rows, lanes])).astype(BF16)
        acc_ref[:, chunk] += jnp.dot(vt_ref[...], a_ref[:, chunk], preferred_element_type=F32)

    @pl.when(sum(n_extra) > 0)
    def _():
        key = lax.broadcasted_iota(jnp.int32, (N_KEYS, LANES), 0).astype(F32)
        sub = lax.broadcasted_iota(jnp.int32, (SUBLANES, LANES), 0)
        a_ref[...] = jnp.zeros_like(a_ref)

        for lc in range(n_lc):
            lanes = slice(lc * LANES, (lc + 1) * LANES)

            def tied(r, carry, lanes=lanes):
                slot = PEER_TOPK - 1 - r
                pick = sub == (slot % SUBLANES)
                for hd in range(PEER_HEADS):
                    grp = pl.ds(pl.multiple_of(hd * PEER_TOPK + (slot // SUBLANES) * SUBLANES, SUBLANES), SUBLANES)
                    e = jnp.sum(jnp.where(pick, ex_ref[grp, lanes], 0.0), axis=0, keepdims=True)
                    gate = jnp.sum(jnp.where(pick, gx_ref[grp, lanes], 0.0), axis=0, keepdims=True)
                    for al in range(n_a):
                        rows = slice(al * N_KEYS, (al + 1) * N_KEYS)
                        b = e - ((j * n_a + al) * N_KEYS).astype(F32)
                        add = jnp.where(key == b, gate, 0.0) * _gelu(st_ref[rows, lanes])
                        a_ref[rows, lanes] = (a_ref[rows, lanes].astype(F32) + add).astype(BF16)
                return carry

            lax.fori_loop(0, n_extra[lc], tied, 0)

        acc_ref[...] += jnp.dot(vt_ref[...], a_ref[...], preferred_element_type=F32)

    @pl.when(j == pl.num_programs(1) - 1)
    def _():
        o_ref[...] = h_ref[...] + _mod_slice(m_ref[...], 5) * acc_ref[...].T


def _peer_dense(h, x, s, e, th, nloop, ex, gx, mod, cond_of_tile, u_bf, vt_bf, layer):
    n_tok, d = h.shape
    n_blocks = u_bf.shape[1] // EB
    return pl.pallas_call(
        _peer_dense_kernel,
        grid_spec=pltpu.PrefetchScalarGridSpec(
            num_scalar_prefetch=2, grid=(n_tok // TD, n_blocks),
            in_specs=[
                pl.BlockSpec((TD, d), lambda i, j, cond, nl: (i, 0)),
                pl.BlockSpec((2 * PEER_HEADS, N_KEYS, TD), lambda i, j, cond, nl: (0, 0, i)),
                pl.BlockSpec((2 * PEER_HEADS, N_KEYS, TD), lambda i, j, cond, nl: (0, 0, i)),
                pl.BlockSpec((PEER_HEADS, TD), lambda i, j, cond, nl: (0, i)),
                pl.BlockSpec((ROWS_PER_TOKEN, TD), lambda i, j, cond, nl: (0, i)),
                pl.BlockSpec((ROWS_PER_TOKEN, TD), lambda i, j, cond, nl: (0, i)),
                pl.BlockSpec((None, EB, d), lambda i, j, cond, nl: (layer, j, 0)),
                pl.BlockSpec((None, d, EB), lambda i, j, cond, nl: (layer, 0, j)),
                pl.BlockSpec((TD, d), lambda i, j, cond, nl: (i, 0)),
                pl.BlockSpec((None, 1, 6 * d), lambda i, j, cond, nl: (cond[i * (TD // TM)], 0, 0)),
            ],
            out_specs=pl.BlockSpec((TD, d), lambda i, j, cond, nl: (i, 0)),
            scratch_shapes=[
                pltpu.VMEM((EB, TD), F32),
                pltpu.VMEM((EB, TD), BF16),
                pltpu.VMEM((d, TD), F32),
            ]),
        out_shape=jax.ShapeDtypeStruct((n_tok, d), F32),
        compiler_params=pltpu.CompilerParams(
            dimension_semantics=("arbitrary", "arbitrary"), vmem_limit_bytes=DENSE_VMEM_LIMIT),
        name="peer_dense",
    )(cond_of_tile, nloop, x, s, e, th, ex, gx, u_bf, vt_bf, h, mod)


def _rope_tables(seq_len):
    rows = seq_len // GRID_W
    row = jnp.repeat(jnp.arange(rows), GRID_W).astype(F32)
    col = jnp.tile(jnp.arange(GRID_W), rows).astype(F32)
    half = QK_ROPE // 2
    inv = ROPE_THETA ** (-(jnp.arange(half // 2, dtype=F32) * 2.0 / half))
    ang = jnp.concatenate([row[:, None] * inv, col[:, None] * inv], axis=-1)
    cos, sin = jnp.cos(ang), jnp.sin(ang)
    cos_pairs = jnp.repeat(cos, 2, axis=-1)
    sin_pairs = jnp.stack([-sin, sin], axis=-1).reshape(seq_len, QK_ROPE)
    cos_t = jnp.ones((TM + seq_len, LANES), F32).at[TM:, QK_NOPE:QK_DIM].set(cos_pairs)
    sin_t = jnp.zeros((TM + seq_len, LANES), F32).at[TM:, QK_NOPE:QK_DIM].set(sin_pairs)
    return cos_t, sin_t


def _pad_heads(w, width):
    k = w.shape[0]
    w3 = w.reshape(k, MLA_HEADS, width)
    return jnp.pad(w3, ((0, 0), (0, 0), (0, LANES - width))).reshape(k, MLA_HEADS * LANES)


def _attn_weights(j, q_a_w, q_a_norm, q_b_w, kv_a_w, kv_a_norm, kv_b_w, q_norm, k_norm, o_w):
    d = q_a_w.shape[1]
    kv_lora = kv_a_norm.shape[1]
    kvb = kv_b_w[j].reshape(kv_lora, MLA_HEADS, QK_NOPE + V_HEAD)
    lane_pad = (0, LANES - QK_DIM)
    return {
        "q_a_w": q_a_w[j].astype(BF16),
        "q_a_norm": q_a_norm[j].reshape(1, -1),
        "q_b_w": _pad_heads(q_b_w[j], QK_DIM).astype(BF16),
        "kv_w": kv_a_w[j][:, :kv_lora].astype(BF16),
        "kv_norm": kv_a_norm[j].reshape(1, -1),
        "kpe_w": jnp.pad(kv_a_w[j][:, kv_lora:], ((0, 0), (QK_NOPE, LANES - QK_DIM))).astype(BF16),
        "k_w": _pad_heads(kvb[:, :, :QK_NOPE].reshape(kv_lora, MLA_HEADS * QK_NOPE), QK_NOPE).astype(BF16),
        "v_w": kvb[:, :, QK_NOPE:].reshape(kv_lora, MLA_HEADS * V_HEAD).astype(BF16),
        "q_norm": jnp.pad(q_norm[j], lane_pad).reshape(1, LANES),
        "k_norm": jnp.pad(k_norm[j], lane_pad).reshape(1, LANES),
        "o_w": o_w[j].astype(BF16),
    }


def kernel(x_prompt, x_sample, cache_ckv, cache_kpe, c, c_ctx, mod_w, mod_b, norm1_g, norm2_g, conv_w_in, conv_b_in, conv_dw, conv_dw_b, conv_norm_g, conv_w_out, conv_b_out, q_a_w, q_a_norm, q_b_w, kv_a_w, kv_a_norm, kv_b_w, q_norm, k_norm, o_w, peer_wq, peer_keys, peer_u, peer_v):
    n_prompt, seq, d = x_prompt.shape
    n_dec, dec_seq, _ = x_sample.shape
    past = cache_ckv.shape[2]
    depth = mod_w.shape[0]
    assert d == D_MODEL and seq % TM == 0 and dec_seq % TM == 0 and past % TM == 0
    assert 1 + n_dec <= 8
    assert (n_prompt * seq) % TD == 0 and dec_seq % TD == 0 and peer_u.shape[1] % EB == 0
    dims = (n_prompt, seq, n_dec, dec_seq, past)

    n_p_rows = n_prompt * seq
    h = jnp.concatenate([x_prompt.reshape(n_p_rows, d), x_sample.reshape(n_dec * dec_seq, d)], axis=0)
    n_tok = h.shape[0]

    tiles_p, tps = n_p_rows // TM, dec_seq // TM
    tile = jnp.arange(n_tok // TM, dtype=jnp.int32)
    in_p = tile < tiles_p
    cond_of_tile = jnp.where(in_p, 0, 1 + (tile - tiles_p) // tps).astype(jnp.int32)
    pos_p, pos_s = tile % (seq // TM), (tile - tiles_p) % tps
    seq_first = jnp.where(in_p, pos_p == 0, pos_s == 0).astype(jnp.int32)
    seq_last = jnp.where(in_p, pos_p == seq // TM - 1, pos_s == tps - 1).astype(jnp.int32)

    cond8 = jnp.zeros((8, d), F32).at[0].set(c_ctx).at[1:1 + n_dec].set(c)
    mod_all = _modulation_all(cond8, mod_w, mod_b)
    u_bf = peer_u.astype(BF16)
    vt_bf = jnp.swapaxes(peer_v.astype(BF16), 1, 2)
    cos_t, sin_t = _rope_tables(dec_seq)
    cache_kpe_pad = jnp.pad(cache_kpe, ((0, 0), (0, 0), (0, 0), (QK_NOPE, LANES - QK_DIM)))

    new_ckv, new_kpe = [], []
    for i in range(depth):
        mod = mod_all[i].reshape(8, 1, 6 * d)
        j = i // 2
        if i % 2 == 0:
            h = _conv_layer(h, mod, cond_of_tile, seq_first, seq_last, norm1_g[i],
                            conv_w_in[j].astype(BF16), conv_b_in[j], conv_dw[j], conv_dw_b[j],
                            conv_norm_g[j], conv_w_out[j].astype(BF16), conv_b_out[j])
        else:
            w = _attn_weights(j, q_a_w, q_a_norm, q_b_w, kv_a_w, kv_a_norm, kv_b_w, q_norm, k_norm, o_w)
            h, ckv_p, kpe_p = _attn_layer(h, mod, cond_of_tile, norm1_g[i], w, cache_ckv[:, j],
                                          cache_kpe_pad[:, j], cos_t, sin_t, dims)
            new_ckv.append(ckv_p.reshape(n_prompt, seq, -1))
            new_kpe.append(kpe_p.reshape(n_prompt, seq, -1))
        x, s, e, th, nx, ex, gx = _peer_route(h, mod, cond_of_tile, norm2_g[i], peer_wq[i].T.astype(BF16),
                                              peer_keys[i].astype(BF16))
        nloop = jnp.max(nx.reshape(PEER_HEADS, n_tok // LANES, LANES), axis=(0, 2)).astype(jnp.int32)
        h = _peer_dense(h, x, s, e, th, nloop, ex, gx, mod, cond_of_tile, u_bf, vt_bf, i)

    y_prompt = h[:n_p_rows].reshape(n_prompt, seq, d)
    y_sample = h[n_p_rows:].reshape(n_dec, dec_seq, d)
    return (y_prompt, y_sample, jnp.stack(new_ckv, axis=1), jnp.stack(new_kpe, axis=1))
```

```python
import functools

import jax
import jax.numpy as jnp
from jax import lax
from jax.experimental import pallas as pl
from jax.experimental.pallas import tpu as pltpu

F32 = jnp.float32
BF16 = jnp.bfloat16

D_MODEL = 1024
GRID_W = 64
CONV_K = 31
MLA_HEADS = 16
QK_NOPE = 64
QK_ROPE = 32
V_HEAD = 64
QK_DIM = QK_NOPE + QK_ROPE
ROPE_THETA = 10000.0
PEER_HEADS = 8
N_KEYS = 128
PEER_TOPK = 16
EPS = 1e-6

LANES = 128
SUBLANES = 8
TM = 256
HALO = 16
TD = 512
EB = 1024
DENSE_VMEM_LIMIT = 56 * 1024 * 1024
VMEM_LIMIT = 48 * 1024 * 1024


def _cparams(n_grid):
    return pltpu.CompilerParams(
        dimension_semantics=("arbitrary",) * n_grid, vmem_limit_bytes=VMEM_LIMIT)


def _rms(x, g, denom=None):
    n = x.shape[-1] if denom is None else denom
    ms = jnp.sum(x * x, axis=-1, keepdims=True) / float(n)
    return x * lax.rsqrt(ms + EPS) * g


def _modulate(h, g, shift, scale):
    return _rms(h, g) * (1.0 + scale) + shift


def _gelu(x):
    return 0.5 * x * (1.0 + lax.erf(x * (0.5 ** 0.5)))


def _mod_slice(m, k):
    return m[:, k * D_MODEL:(k + 1) * D_MODEL]


def _mod_kernel(c_ref, w_ref, b_ref, o_ref):
    c = c_ref[...]
    a = (c * jax.nn.sigmoid(c)).astype(BF16)
    o_ref[...] = jnp.dot(a, w_ref[...].astype(BF16), preferred_element_type=F32) + b_ref[...]


def _modulation_all(cond8, mod_w, mod_b):
    depth, d, n6 = mod_w.shape
    tn = 1536
    return pl.pallas_call(
        _mod_kernel,
        grid=(depth, n6 // tn),
        in_specs=[
            pl.BlockSpec((8, d), lambda l, n: (0, 0)),
            pl.BlockSpec((None, d, tn), lambda l, n: (l, 0, n)),
            pl.BlockSpec((None, 1, tn), lambda l, n: (l, 0, n)),
        ],
        out_specs=pl.BlockSpec((None, 8, tn), lambda l, n: (l, 0, n)),
        out_shape=jax.ShapeDtypeStruct((depth, 8, n6), F32),
        compiler_params=_cparams(2),
        name="modulation",
    )(cond8, mod_w, mod_b.reshape(depth, 1, n6))


def _conv_in_kernel(cond_ref, h_ref, m_ref, g_ref, w_ref, b_ref, o_ref):
    del cond_ref
    m = m_ref[...]
    n = _modulate(h_ref[...], g_ref[...], _mod_slice(m, 0), _mod_slice(m, 1))
    hh = jnp.dot(n.astype(BF16), w_ref[...], preferred_element_type=F32) + b_ref[...]
    o_ref[...] = hh[:, :D_MODEL] * jax.nn.sigmoid(hh[:, D_MODEL:])


def _conv_out_kernel(first_ref, last_ref, cond_ref, cur_ref, prev_ref, next_ref, dw_ref, dwb_ref, ng_ref,
                     w_ref, b_ref, h_ref, m_ref, o_ref, pad_ref, conv_ref):
    del cond_ref
    i = pl.program_id(0)
    keep_prev = jnp.where(first_ref[i] == 0, 1.0, 0.0)
    keep_next = jnp.where(last_ref[i] == 0, 1.0, 0.0)
    pad_ref[0:HALO, :] = prev_ref[...] * keep_prev
    pad_ref[HALO:HALO + TM, :] = cur_ref[...]
    pad_ref[HALO + TM:HALO + TM + HALO, :] = next_ref[...] * keep_next
    base = HALO - CONV_K // 2
    for c in range(D_MODEL // LANES):
        cols = slice(c * LANES, (c + 1) * LANES)
        acc = jnp.zeros((TM, LANES), F32) + dwb_ref[:, cols]
        for k in range(CONV_K):
            acc = acc + dw_ref[k:k + 1, cols] * pad_ref[base + k:base + k + TM, cols]
        conv_ref[:, cols] = acc
    y = _rms(conv_ref[...], ng_ref[...])
    y = y * jax.nn.sigmoid(y)
    out = jnp.dot(y.astype(BF16), w_ref[...], preferred_element_type=F32) + b_ref[...]
    o_ref[...] = h_ref[...] + _mod_slice(m_ref[...], 2) * out


def _conv_layer(h, mod, cond_of_tile, seq_first, seq_last, g, w_in, b_in, dw, dw_b, ng, w_out, b_out):
    n_tok, d = h.shape
    nt = n_tok // TM
    hb = TM // HALO
    glu = pl.pallas_call(
        _conv_in_kernel,
        grid_spec=pltpu.PrefetchScalarGridSpec(
            num_scalar_prefetch=1, grid=(nt,),
            in_specs=[
                pl.BlockSpec((TM, d), lambda i, cond: (i, 0)),
                pl.BlockSpec((None, 1, 6 * d), lambda i, cond: (cond[i], 0, 0)),
                pl.BlockSpec((1, d), lambda i, cond: (0, 0)),
                pl.BlockSpec((d, 2 * d), lambda i, cond: (0, 0)),
                pl.BlockSpec((1, 2 * d), lambda i, cond: (0, 0)),
            ],
            out_specs=pl.BlockSpec((TM, d), lambda i, cond: (i, 0))),
        out_shape=jax.ShapeDtypeStruct((n_tok, d), F32),
        compiler_params=_cparams(1),
        name="conv_in",
    )(cond_of_tile, h, mod, g.reshape(1, d), w_in, b_in.reshape(1, 2 * d))

    n_hb = n_tok // HALO
    dw_pad = jnp.zeros((32, d), F32).at[:CONV_K].set(dw)
    return pl.pallas_call(
        _conv_out_kernel,
        grid_spec=pltpu.PrefetchScalarGridSpec(
            num_scalar_prefetch=3, grid=(nt,),
            in_specs=[
                pl.BlockSpec((TM, d), lambda i, f, l, cond: (i, 0)),
                pl.BlockSpec((HALO, d), lambda i, f, l, cond: (jnp.maximum(i * hb - 1, 0), 0)),
                pl.BlockSpec((HALO, d), lambda i, f, l, cond: (jnp.minimum((i + 1) * hb, n_hb - 1), 0)),
                pl.BlockSpec((32, d), lambda i, f, l, cond: (0, 0)),
                pl.BlockSpec((1, d), lambda i, f, l, cond: (0, 0)),
                pl.BlockSpec((1, d), lambda i, f, l, cond: (0, 0)),
                pl.BlockSpec((d, d), lambda i, f, l, cond: (0, 0)),
                pl.BlockSpec((1, d), lambda i, f, l, cond: (0, 0)),
                pl.BlockSpec((TM, d), lambda i, f, l, cond: (i, 0)),
                pl.BlockSpec((None, 1, 6 * d), lambda i, f, l, cond: (cond[i], 0, 0)),
            ],
            out_specs=pl.BlockSpec((TM, d), lambda i, f, l, cond: (i, 0)),
            scratch_shapes=[pltpu.VMEM((TM + 2 * HALO, d), F32), pltpu.VMEM((TM, d), F32)]),
        out_shape=jax.ShapeDtypeStruct((n_tok, d), F32),
        compiler_params=_cparams(1),
        name="conv_out",
    )(seq_first, seq_last, cond_of_tile, glu, glu, glu, dw_pad, dw_b.reshape(1, d), ng.reshape(1, d),
      w_out, b_out.reshape(1, d), h, mod)


def _rope(y, cos, sin):
    lane = lax.broadcasted_iota(jnp.int32, y.shape, 1)
    nxt = pltpu.roll(y, LANES - 1, 1)
    prv = pltpu.roll(y, 1, 1)
    partner = jnp.where((lane & 1) == 0, nxt, prv)
    return y * cos + partner * sin


def _attn_latent_kernel(cond_ref, h_ref, m_ref, g_ref, qaw_ref, qan_ref, kvw_ref, kvn_ref, kpew_ref,
                        qlat_ref, ckv_ref, kpe_ref):
    del cond_ref
    m = m_ref[...]
    n = _modulate(h_ref[...], g_ref[...], _mod_slice(m, 0), _mod_slice(m, 1)).astype(BF16)
    qa = jnp.dot(n, qaw_ref[...], preferred_element_type=F32)
    qlat_ref[...] = _rms(qa, qan_ref[...]).astype(BF16)
    kv = jnp.dot(n, kvw_ref[...], preferred_element_type=F32)
    ckv_ref[...] = _rms(kv, kvn_ref[...])
    kpe_ref[...] = jnp.dot(n, kpew_ref[...], preferred_element_type=F32)


def _q_kernel(qlat_ref, w_ref, qn_ref, cos_ref, sin_ref, o_ref):
    q = jnp.dot(qlat_ref[...], w_ref[...], preferred_element_type=F32)
    cos = cos_ref[...]
    sin = sin_ref[...]
    qn = qn_ref[...]
    for hd in range(MLA_HEADS):
        cols = slice(hd * LANES, (hd + 1) * LANES)
        y = _rms(q[:, cols], qn, denom=QK_DIM)
        o_ref[:, cols] = _rope(y, cos, sin).astype(BF16)


def _kv_kernel(ckv_ref, kpe_ref, wk_ref, wv_ref, kn_ref, cos_ref, sin_ref, k_ref, v_ref):
    c = ckv_ref[...].astype(BF16)
    kf = jnp.dot(c, wk_ref[...], preferred_element_type=F32)
    v_ref[...] = jnp.dot(c, wv_ref[...], preferred_element_type=F32).astype(BF16)
    kpe = kpe_ref[...]
    cos = cos_ref[...]
    sin = sin_ref[...]
    kn = kn_ref[...]
    for hd in range(MLA_HEADS):
        cols = slice(hd * LANES, (hd + 1) * LANES)
        y = _rms(kf[:, cols] + kpe, kn, denom=QK_DIM)
        k_ref[:, cols] = _rope(y, cos, sin).astype(BF16)


def _attn_kernel(q_ref, k_ref, v_ref, o_ref):
    scale = float(QK_DIM) ** -0.5
    for hh in range(2):
        q = q_ref[:, hh * LANES:(hh + 1) * LANES]
        k = k_ref[:, hh * LANES:(hh + 1) * LANES]
        s = lax.dot_general(q, k, (((1,), (1,)), ((), ())), preferred_element_type=F32) * scale
        p = jnp.exp(s - jnp.max(s, axis=-1, keepdims=True))
        l = jnp.sum(p, axis=-1, keepdims=True)
        o = jnp.dot(p.astype(BF16), v_ref[:, hh * V_HEAD:(hh + 1) * V_HEAD], preferred_element_type=F32)
        o_ref[:, hh * V_HEAD:(hh + 1) * V_HEAD] = (o / l).astype(BF16)


def _attn_out_kernel(cond_ref, op_ref, os_ref, w_ref, h_ref, m_ref, o_ref, *, n_prompt_tiles):
    del cond_ref
    i = pl.program_id(0)
    o = jnp.where(i < n_prompt_tiles, op_ref[...], os_ref[...])
    out = jnp.dot(o, w_ref[...], preferred_element_type=F32)
    o_ref[...] = h_ref[...] + _mod_slice(m_ref[...], 2) * out


def _attention(q, k, v, n_batch, sq, sk, q_tile_offset):
    nq = sq // TM
    hp = MLA_HEADS // 2
    return pl.pallas_call(
        _attn_kernel,
        grid=(n_batch, hp, nq),
        in_specs=[
            pl.BlockSpec((TM, 2 * LANES), lambda b, h, qi: (q_tile_offset + b * nq + qi, h)),
            pl.BlockSpec((sk, 2 * LANES), lambda b, h, qi: (b, h)),
            pl.BlockSpec((sk, 2 * V_HEAD), lambda b, h, qi: (b, h)),
        ],
        out_specs=pl.BlockSpec((TM, 2 * V_HEAD), lambda b, h, qi: (b * nq + qi, h)),
        out_shape=jax.ShapeDtypeStruct((n_batch * sq, MLA_HEADS * V_HEAD), BF16),
        compiler_params=_cparams(3),
        name="attention",
    )(q, k, v)


def _kv_expand(ckv, kpe, wk, wv, kn, cos_t, sin_t, table_block):
    n_rows = ckv.shape[0]
    nt = n_rows // TM
    return pl.pallas_call(
        _kv_kernel,
        grid=(nt,),
        in_specs=[
            pl.BlockSpec((TM, ckv.shape[1]), lambda i: (i, 0)),
            pl.BlockSpec((TM, LANES), lambda i: (i, 0)),
            pl.BlockSpec(wk.shape, lambda i: (0, 0)),
            pl.BlockSpec(wv.shape, lambda i: (0, 0)),
            pl.BlockSpec((1, LANES), lambda i: (0, 0)),
            pl.BlockSpec((TM, LANES), lambda i: (table_block(i), 0)),
            pl.BlockSpec((TM, LANES), lambda i: (table_block(i), 0)),
        ],
        out_specs=[pl.BlockSpec((TM, MLA_HEADS * LANES), lambda i: (i, 0)),
                   pl.BlockSpec((TM, MLA_HEADS * V_HEAD), lambda i: (i, 0))],
        out_shape=[jax.ShapeDtypeStruct((n_rows, MLA_HEADS * LANES), BF16),
                   jax.ShapeDtypeStruct((n_rows, MLA_HEADS * V_HEAD), BF16)],
        compiler_params=_cparams(1),
        name="kv_expand",
    )(ckv, kpe, wk, wv, kn, cos_t, sin_t)


def _attn_layer(h, mod, cond_of_tile, g, w, cache_ckv, cache_kpe, cos_t, sin_t, dims):
    n_tok, d = h.shape
    nt = n_tok // TM
    n_prompt, seq, n_dec, dec_seq, past = dims
    npt = n_prompt * seq // TM
    tps = dec_seq // TM
    q_lora = w["q_a_w"].shape[1]
    kv_lora = w["kv_w"].shape[1]

    qlat, ckv, kpe = pl.pallas_call(
        _attn_latent_kernel,
        grid_spec=pltpu.PrefetchScalarGridSpec(
            num_scalar_prefetch=1, grid=(nt,),
            in_specs=[
                pl.BlockSpec((TM, d), lambda i, cond: (i, 0)),
                pl.BlockSpec((None, 1, 6 * d), lambda i, cond: (cond[i], 0, 0)),
                pl.BlockSpec((1, d), lambda i, cond: (0, 0)),
                pl.BlockSpec((d, q_lora), lambda i, cond: (0, 0)),
                pl.BlockSpec((1, q_lora), lambda i, cond: (0, 0)),
                pl.BlockSpec((d, kv_lora), lambda i, cond: (0, 0)),
                pl.BlockSpec((1, kv_lora), lambda i, cond: (0, 0)),
                pl.BlockSpec((d, LANES), lambda i, cond: (0, 0)),
            ],
            out_specs=[pl.BlockSpec((TM, q_lora), lambda i, cond: (i, 0)),
                       pl.BlockSpec((TM, kv_lora), lambda i, cond: (i, 0)),
                       pl.BlockSpec((TM, LANES), lambda i, cond: (i, 0))]),
        out_shape=[jax.ShapeDtypeStruct((n_tok, q_lora), BF16),
                   jax.ShapeDtypeStruct((n_tok, kv_lora), F32),
                   jax.ShapeDtypeStruct((n_tok, LANES), F32)],
        compiler_params=_cparams(1),
        name="attn_latent",
    )(cond_of_tile, h, mod, g.reshape(1, d), w["q_a_w"], w["q_a_norm"], w["kv_w"], w["kv_norm"], w["kpe_w"])

    q_table = lambda i: jnp.where(i < npt, 0, 1 + (i - npt) % tps)
    q = pl.pallas_call(
        _q_kernel,
        grid=(nt,),
        in_specs=[
            pl.BlockSpec((TM, q_lora), lambda i: (i, 0)),
            pl.BlockSpec(w["q_b_w"].shape, lambda i: (0, 0)),
            pl.BlockSpec((1, LANES), lambda i: (0, 0)),
            pl.BlockSpec((TM, LANES), lambda i: (q_table(i), 0)),
            pl.BlockSpec((TM, LANES), lambda i: (q_table(i), 0)),
        ],
        out_specs=pl.BlockSpec((TM, MLA_HEADS * LANES), lambda i: (i, 0)),
        out_shape=jax.ShapeDtypeStruct((n_tok, MLA_HEADS * LANES), BF16),
        compiler_params=_cparams(1),
        name="q_expand",
    )(qlat, w["q_b_w"], w["q_norm"], cos_t, sin_t)

    n_p_rows = n_prompt * seq
    ckv_p, kpe_p = ckv[:n_p_rows], kpe[:n_p_rows]
    k_p, v_p = _kv_expand(ckv_p, kpe_p, w["k_w"], w["v_w"], w["k_norm"], cos_t, sin_t, lambda i: 0)
    ckv_s = jnp.concatenate([cache_ckv, ckv[n_p_rows:].reshape(n_dec, dec_seq, kv_lora)], axis=1)
    kpe_s = jnp.concatenate([cache_kpe, kpe[n_p_rows:].reshape(n_dec, dec_seq, LANES)], axis=1)
    sk = past + dec_seq
    tpk = sk // TM
    pc = past // TM
    k_s, v_s = _kv_expand(ckv_s.reshape(n_dec * sk, kv_lora), kpe_s.reshape(n_dec * sk, LANES),
                          w["k_w"], w["v_w"], w["k_norm"], cos_t, sin_t,
                          lambda i: jnp.where(i % tpk < pc, 0, 1 + i % tpk - pc))

    o_p = _attention(q, k_p, v_p, n_prompt, seq, seq, 0)
    o_s = _attention(q, k_s, v_s, n_dec, dec_seq, sk, npt)

    h_new = pl.pallas_call(
        functools.partial(_attn_out_kernel, n_prompt_tiles=npt),
        grid_spec=pltpu.PrefetchScalarGridSpec(
            num_scalar_prefetch=1, grid=(nt,),
            in_specs=[
                pl.BlockSpec((TM, d), lambda i, cond: (jnp.minimum(i, npt - 1), 0)),
                pl.BlockSpec((TM, d), lambda i, cond: (jnp.maximum(i - npt, 0), 0)),
                pl.BlockSpec((d, d), lambda i, cond: (0, 0)),
                pl.BlockSpec((TM, d), lambda i, cond: (i, 0)),
                pl.BlockSpec((None, 1, 6 * d), lambda i, cond: (cond[i], 0, 0)),
            ],
            out_specs=pl.BlockSpec((TM, d), lambda i, cond: (i, 0))),
        out_shape=jax.ShapeDtypeStruct((n_tok, d), F32),
        compiler_params=_cparams(1),
        name="attn_out",
    )(cond_of_tile, o_p, o_s, w["o_w"], h, mod)
    return h_new, ckv_p, kpe_p[:, QK_NOPE:QK_NOPE + QK_ROPE]


def _topk_rows(s, k):
    n = s.shape[0]
    row = lax.broadcasted_iota(jnp.int32, s.shape, 0).astype(F32)
    rank = jnp.full(s.shape, float(k), F32)
    vals = []
    for r in range(k):
        m = jnp.max(s, axis=0, keepdims=True)
        pos = jnp.min(jnp.where(s == m, row, float(n)), axis=0, keepdims=True)
        hit = row == pos
        vals.append(m)
        s = jnp.where(hit, -jnp.inf, s)
        rank = jnp.where(hit, float(r), rank)
    return jnp.concatenate(vals, axis=0), rank


def _peer_route_kernel(cond_ref, h_ref, m_ref, g_ref, wqt_ref, keys_ref,
                       x_ref, r1_ref, cut_ref, e_ref,
                       qt_ref, sv_ref, r0_ref):
    del cond_ref
    m = m_ref[...]
    n = _modulate(h_ref[...], g_ref[...], _mod_slice(m, 3), _mod_slice(m, 4)).astype(BF16)
    x_ref[...] = n
    qt_ref[...] = lax.dot_general(wqt_ref[...], n, (((1,), (1,)), ((), ())), preferred_element_type=F32)

    def sub_keys(hd, carry):
        for p, rank_ref in enumerate((r0_ref, r1_ref)):
            hp = 2 * hd + p
            q = qt_ref[pl.ds(pl.multiple_of(hp * N_KEYS, N_KEYS), N_KEYS), :].astype(BF16)
            s = jnp.dot(keys_ref[p], q, preferred_element_type=F32)
            vals, rank = _topk_rows(s, PEER_TOPK)
            sv_ref[hp] = vals
            rank_ref[hd] = rank
            e_ref[hp] = jnp.exp(s - vals[0:1])
        return carry

    lax.fori_loop(0, PEER_HEADS, sub_keys, 0)

    def head(hd, carry):
        sv0, sv1 = sv_ref[2 * hd], sv_ref[2 * hd + 1]
        widths = [min(PEER_TOPK, (PEER_TOPK + 1) // (a + 1)) for a in range(PEER_TOPK)]
        pad = -sum(widths) % SUBLANES
        cand = jnp.concatenate([sv0[a:a + 1] + sv1[0:w] for a, w in enumerate(widths)]
                               + [jnp.full((pad, TM), -jnp.inf, F32)], axis=0)
        top_s, rank = _topk_rows(cand, PEER_TOPK)
        p = jnp.exp(top_s - top_s[0:1])
        e_ref[2 * hd + 1] = e_ref[2 * hd + 1] / jnp.sum(p, axis=0, keepdims=True)
        chosen = jnp.where(rank < float(PEER_TOPK), 1.0, 0.0)
        r0 = r0_ref[hd]
        cut = jnp.zeros((N_KEYS, TM), F32)
        start = 0
        for a, w in enumerate(widths):
            count = jnp.sum(chosen[start:start + w], axis=0, keepdims=True)
            cut = jnp.where(r0 == float(a), count, cut)
            start += w
        cut_ref[hd] = cut
        return carry

    lax.fori_loop(0, PEER_HEADS, head, 0)


def _peer_route(h, mod, cond_of_tile, g, wqt, keys):
    n_tok, d = h.shape
    nt = n_tok // TM
    nq = wqt.shape[0]
    return pl.pallas_call(
        _peer_route_kernel,
        grid_spec=pltpu.PrefetchScalarGridSpec(
            num_scalar_prefetch=1, grid=(nt,),
            in_specs=[
                pl.BlockSpec((TM, d), lambda i, cond: (i, 0)),
                pl.BlockSpec((None, 1, 6 * d), lambda i, cond: (cond[i], 0, 0)),
                pl.BlockSpec((1, d), lambda i, cond: (0, 0)),
                pl.BlockSpec((nq, d), lambda i, cond: (0, 0)),
                pl.BlockSpec(keys.shape, lambda i, cond: (0, 0, 0)),
            ],
            out_specs=[pl.BlockSpec((TM, d), lambda i, cond: (i, 0)),
                       pl.BlockSpec((PEER_HEADS, N_KEYS, TM), lambda i, cond: (0, 0, i)),
                       pl.BlockSpec((PEER_HEADS, N_KEYS, TM), lambda i, cond: (0, 0, i)),
                       pl.BlockSpec((2 * PEER_HEADS, N_KEYS, TM), lambda i, cond: (0, 0, i))],
            scratch_shapes=[
                pltpu.VMEM((nq, TM), F32),
                pltpu.VMEM((2 * PEER_HEADS, PEER_TOPK, TM), F32),
                pltpu.VMEM((PEER_HEADS, N_KEYS, TM), F32),
            ]),
        out_shape=[jax.ShapeDtypeStruct((n_tok, d), BF16),
                   jax.ShapeDtypeStruct((PEER_HEADS, N_KEYS, n_tok), F32),
                   jax.ShapeDtypeStruct((PEER_HEADS, N_KEYS, n_tok), F32),
                   jax.ShapeDtypeStruct((2 * PEER_HEADS, N_KEYS, n_tok), F32)],
        compiler_params=_cparams(1),
        name="peer_route",
    )(cond_of_tile, h, mod, g.reshape(1, d), wqt, keys)


def _peer_dense_kernel(cond_ref, x_ref, r1_ref, cut_ref, e_ref, u_ref, vt_ref, h_ref, m_ref, o_ref,
                       st_ref, a_ref, acc_ref):
    del cond_ref
    j = pl.program_id(1)
    n_a = EB // N_KEYS
    assert n_a == SUBLANES
    a_rows = pl.ds(pl.multiple_of(j * n_a, n_a), n_a)

    @pl.when(j == 0)
    def _():
        acc_ref[...] = jnp.zeros_like(acc_ref)

    st_ref[...] = lax.dot_general(u_ref[...], x_ref[...], (((1,), (1,)), ((), ())), preferred_element_type=F32)
    for lc in range(TD // LANES):
        lanes = slice(lc * LANES, (lc + 1) * LANES)
        for al in range(n_a):
            rows = slice(al * N_KEYS, (al + 1) * N_KEYS)
            g = jnp.zeros((N_KEYS, LANES), F32)
            for hd in range(PEER_HEADS):
                cut = cut_ref[hd, a_rows, lanes][al:al + 1]
                e0 = e_ref[2 * hd, a_rows, lanes][al:al + 1]
                wgt = e0 * e_ref[2 * hd + 1, :, lanes]
                g = g + jnp.where(r1_ref[hd, :, lanes] < cut, wgt, 0.0)
            a_ref[rows, lanes] = (g * _gelu(st_ref[rows, lanes])).astype(BF16)
    acc_ref[...] += jnp.dot(vt_ref[...], a_ref[...], preferred_element_type=F32)

    @pl.when(j == pl.num_programs(1) - 1)
    def _():
        o_ref[...] = h_ref[...] + _mod_slice(m_ref[...], 5) * acc_ref[...].T


def _peer_dense(h, x, r1, cut, e, mod, cond_of_tile, u_bf, vt_bf, layer):
    n_tok, d = h.shape
    n_blocks = u_bf.shape[1] // EB
    return pl.pallas_call(
        _peer_dense_kernel,
        grid_spec=pltpu.PrefetchScalarGridSpec(
            num_scalar_prefetch=1, grid=(n_tok // TD, n_blocks),
            in_specs=[
                pl.BlockSpec((TD, d), lambda i, j, cond: (i, 0)),
                pl.BlockSpec((PEER_HEADS, N_KEYS, TD), lambda i, j, cond: (0, 0, i)),
                pl.BlockSpec((PEER_HEADS, N_KEYS, TD), lambda i, j, cond: (0, 0, i)),
                pl.BlockSpec((2 * PEER_HEADS, N_KEYS, TD), lambda i, j, cond: (0, 0, i)),
                pl.BlockSpec((None, EB, d), lambda i, j, cond: (layer, j, 0)),
                pl.BlockSpec((None, d, EB), lambda i, j, cond: (layer, 0, j)),
                pl.BlockSpec((TD, d), lambda i, j, cond: (i, 0)),
                pl.BlockSpec((None, 1, 6 * d), lambda i, j, cond: (cond[i * (TD // TM)], 0, 0)),
            ],
            out_specs=pl.BlockSpec((TD, d), lambda i, j, cond: (i, 0)),
            scratch_shapes=[
                pltpu.VMEM((EB, TD), F32),
                pltpu.VMEM((EB, TD), BF16),
                pltpu.VMEM((d, TD), F32),
            ]),
        out_shape=jax.ShapeDtypeStruct((n_tok, d), F32),
        compiler_params=pltpu.CompilerParams(
            dimension_semantics=("arbitrary", "arbitrary"), vmem_limit_bytes=DENSE_VMEM_LIMIT),
        name="peer_dense",
    )(cond_of_tile, x, r1, cut, e, u_bf, vt_bf, h, mod)


def _rope_tables(seq_len):
    rows = seq_len // GRID_W
    row = jnp.repeat(jnp.arange(rows), GRID_W).astype(F32)
    col = jnp.tile(jnp.arange(GRID_W), rows).astype(F32)
    half = QK_ROPE // 2
    inv = ROPE_THETA ** (-(jnp.arange(half // 2, dtype=F32) * 2.0 / half))
    ang = jnp.concatenate([row[:, None] * inv, col[:, None] * inv], axis=-1)
    cos, sin = jnp.cos(ang), jnp.sin(ang)
    cos_pairs = jnp.repeat(cos, 2, axis=-1)
    sin_pairs = jnp.stack([-sin, sin], axis=-1).reshape(seq_len, QK_ROPE)
    cos_t = jnp.ones((TM + seq_len, LANES), F32).at[TM:, QK_NOPE:QK_DIM].set(cos_pairs)
    sin_t = jnp.zeros((TM + seq_len, LANES), F32).at[TM:, QK_NOPE:QK_DIM].set(sin_pairs)
    return cos_t, sin_t


def _pad_heads(w, width):
    k = w.shape[0]
    w3 = w.reshape(k, MLA_HEADS, width)
    return jnp.pad(w3, ((0, 0), (0, 0), (0, LANES - width))).reshape(k, MLA_HEADS * LANES)


def _attn_weights(j, q_a_w, q_a_norm, q_b_w, kv_a_w, kv_a_norm, kv_b_w, q_norm, k_norm, o_w):
    kv_lora = kv_a_norm.shape[1]
    kvb = kv_b_w[j].reshape(kv_lora, MLA_HEADS, QK_NOPE + V_HEAD)
    lane_pad = (0, LANES - QK_DIM)
    return {
        "q_a_w": q_a_w[j].astype(BF16),
        "q_a_norm": q_a_norm[j].reshape(1, -1),
        "q_b_w": _pad_heads(q_b_w[j], QK_DIM).astype(BF16),
        "kv_w": kv_a_w[j][:, :kv_lora].astype(BF16),
        "kv_norm": kv_a_norm[j].reshape(1, -1),
        "kpe_w": jnp.pad(kv_a_w[j][:, kv_lora:], ((0, 0), (QK_NOPE, LANES - QK_DIM))).astype(BF16),
        "k_w": _pad_heads(kvb[:, :, :QK_NOPE].reshape(kv_lora, MLA_HEADS * QK_NOPE), QK_NOPE).astype(BF16),
        "v_w": kvb[:, :, QK_NOPE:].reshape(kv_lora, MLA_HEADS * V_HEAD).astype(BF16),
        "q_norm": jnp.pad(q_norm[j], lane_pad).reshape(1, LANES),
        "k_norm": jnp.pad(k_norm[j], lane_pad).reshape(1, LANES),
        "o_w": o_w[j].astype(BF16),
    }


def kernel(x_prompt, x_sample, cache_ckv, cache_kpe, c, c_ctx, mod_w, mod_b, norm1_g, norm2_g, conv_w_in, conv_b_in, conv_dw, conv_dw_b, conv_norm_g, conv_w_out, conv_b_out, q_a_w, q_a_norm, q_b_w, kv_a_w, kv_a_norm, kv_b_w, q_norm, k_norm, o_w, peer_wq, peer_keys, peer_u, peer_v):
    n_prompt, seq, d = x_prompt.shape
    n_dec, dec_seq, _ = x_sample.shape
    past = cache_ckv.shape[2]
    depth = mod_w.shape[0]
    assert d == D_MODEL and seq % TM == 0 and dec_seq % TM == 0 and past % TM == 0
    assert 1 + n_dec <= 8
    assert (n_prompt * seq) % TD == 0 and dec_seq % TD == 0 and peer_u.shape[1] % EB == 0
    dims = (n_prompt, seq, n_dec, dec_seq, past)

    n_p_rows = n_prompt * seq
    h = jnp.concatenate([x_prompt.reshape(n_p_rows, d), x_sample.reshape(n_dec * dec_seq, d)], axis=0)
    n_tok = h.shape[0]

    tiles_p, tps = n_p_rows // TM, dec_seq // TM
    tile = jnp.arange(n_tok // TM, dtype=jnp.int32)
    in_p = tile < tiles_p
    cond_of_tile = jnp.where(in_p, 0, 1 + (tile - tiles_p) // tps).astype(jnp.int32)
    pos_p, pos_s = tile % (seq // TM), (tile - tiles_p) % tps
    seq_first = jnp.where(in_p, pos_p == 0, pos_s == 0).astype(jnp.int32)
    seq_last = jnp.where(in_p, pos_p == seq // TM - 1, pos_s == tps - 1).astype(jnp.int32)

    cond8 = jnp.zeros((8, d), F32).at[0].set(c_ctx).at[1:1 + n_dec].set(c)
    mod_all = _modulation_all(cond8, mod_w, mod_b)
    u_bf = peer_u.astype(BF16)
    vt_bf = jnp.swapaxes(peer_v.astype(BF16), 1, 2)
    cos_t, sin_t = _rope_tables(dec_seq)
    cache_kpe_pad = jnp.pad(cache_kpe, ((0, 0), (0, 0), (0, 0), (QK_NOPE, LANES - QK_DIM)))

    new_ckv, new_kpe = [], []
    for i in range(depth):
        mod = mod_all[i].reshape(8, 1, 6 * d)
        j = i // 2
        if i % 2 == 0:
            h = _conv_layer(h, mod, cond_of_tile, seq_first, seq_last, norm1_g[i],
                            conv_w_in[j].astype(BF16), conv_b_in[j], conv_dw[j], conv_dw_b[j],
                            conv_norm_g[j], conv_w_out[j].astype(BF16), conv_b_out[j])
        else:
            w = _attn_weights(j, q_a_w, q_a_norm, q_b_w, kv_a_w, kv_a_norm, kv_b_w, q_norm, k_norm, o_w)
            h, ckv_p, kpe_p = _attn_layer(h, mod, cond_of_tile, norm1_g[i], w, cache_ckv[:, j],
                                          cache_kpe_pad[:, j], cos_t, sin_t, dims)
            new_ckv.append(ckv_p.reshape(n_prompt, seq, -1))
            new_kpe.append(kpe_p.reshape(n_prompt, seq, -1))
        x, r1, cut, e = _peer_route(h, mod, cond_of_tile, norm2_g[i], peer_wq[i].T.astype(BF16),
                                    peer_keys[i].astype(BF16))
        h = _peer_dense(h, x, r1, cut, e, mod, cond_of_tile, u_bf, vt_bf, i)

    y_prompt = h[:n_p_rows].reshape(n_prompt, seq, d)
    y_sample = h[n_p_rows:].reshape(n_dec, dec_seq, d)
    return (y_prompt, y_sample, jnp.stack(new_ckv, axis=1), jnp.stack(new_kpe, axis=1))
```

```python
import functools

import jax
import jax.numpy as jnp
from jax import lax
from jax.experimental import pallas as pl
from jax.experimental.pallas import tpu as pltpu

F32 = jnp.float32
BF16 = jnp.bfloat16

D_MODEL = 1024
GRID_W = 64
CONV_K = 31
MLA_HEADS = 16
QK_NOPE = 64
QK_ROPE = 32
V_HEAD = 64
QK_DIM = QK_NOPE + QK_ROPE
ROPE_THETA = 10000.0
PEER_HEADS = 8
N_KEYS = 128
PEER_TOPK = 16
EPS = 1e-6

LANES = 128
SUBLANES = 8
TM = 256
HALO = 16
TD = 512
TR = 512
EB = 1024
DENSE_VMEM_LIMIT = 56 * 1024 * 1024
VMEM_LIMIT = 48 * 1024 * 1024


def _cparams(n_grid):
    return pltpu.CompilerParams(
        dimension_semantics=("arbitrary",) * n_grid, vmem_limit_bytes=VMEM_LIMIT)


def _rms(x, g, denom=None):
    n = x.shape[-1] if denom is None else denom
    ms = jnp.sum(x * x, axis=-1, keepdims=True) / float(n)
    return x * lax.rsqrt(ms + EPS) * g


def _modulate(h, g, shift, scale):
    return _rms(h, g) * (1.0 + scale) + shift


def _gelu(x):
    return 0.5 * x * (1.0 + lax.erf(x * (0.5 ** 0.5)))


def _mod_slice(m, k):
    return m[:, k * D_MODEL:(k + 1) * D_MODEL]


def _mod_kernel(c_ref, w_ref, b_ref, o_ref):
    c = c_ref[...]
    a = (c * jax.nn.sigmoid(c)).astype(BF16)
    o_ref[...] = jnp.dot(a, w_ref[...].astype(BF16), preferred_element_type=F32) + b_ref[...]


def _modulation_all(cond8, mod_w, mod_b):
    depth, d, n6 = mod_w.shape
    tn = 1536
    return pl.pallas_call(
        _mod_kernel,
        grid=(depth, n6 // tn),
        in_specs=[
            pl.BlockSpec((8, d), lambda l, n: (0, 0)),
            pl.BlockSpec((None, d, tn), lambda l, n: (l, 0, n)),
            pl.BlockSpec((None, 1, tn), lambda l, n: (l, 0, n)),
        ],
        out_specs=pl.BlockSpec((None, 8, tn), lambda l, n: (l, 0, n)),
        out_shape=jax.ShapeDtypeStruct((depth, 8, n6), F32),
        compiler_params=_cparams(2),
        name="modulation",
    )(cond8, mod_w, mod_b.reshape(depth, 1, n6))


def _conv_in_kernel(cond_ref, h_ref, m_ref, g_ref, w_ref, b_ref, o_ref):
    del cond_ref
    m = m_ref[...]
    n = _modulate(h_ref[...], g_ref[...], _mod_slice(m, 0), _mod_slice(m, 1))
    hh = jnp.dot(n.astype(BF16), w_ref[...], preferred_element_type=F32) + b_ref[...]
    o_ref[...] = hh[:, :D_MODEL] * jax.nn.sigmoid(hh[:, D_MODEL:])


def _conv_out_kernel(first_ref, last_ref, cond_ref, cur_ref, prev_ref, next_ref, dw_ref, dwb_ref, ng_ref,
                     w_ref, b_ref, h_ref, m_ref, o_ref, pad_ref, conv_ref):
    del cond_ref
    i = pl.program_id(0)
    keep_prev = jnp.where(first_ref[i] == 0, 1.0, 0.0)
    keep_next = jnp.where(last_ref[i] == 0, 1.0, 0.0)
    pad_ref[0:HALO, :] = prev_ref[...] * keep_prev
    pad_ref[HALO:HALO + TM, :] = cur_ref[...]
    pad_ref[HALO + TM:HALO + TM + HALO, :] = next_ref[...] * keep_next
    base = HALO - CONV_K // 2
    for c in range(D_MODEL // LANES):
        cols = slice(c * LANES, (c + 1) * LANES)
        acc = jnp.zeros((TM, LANES), F32) + dwb_ref[:, cols]
        for k in range(CONV_K):
            acc = acc + dw_ref[k:k + 1, cols] * pad_ref[base + k:base + k + TM, cols]
        conv_ref[:, cols] = acc
    y = _rms(conv_ref[...], ng_ref[...])
    y = y * jax.nn.sigmoid(y)
    out = jnp.dot(y.astype(BF16), w_ref[...], preferred_element_type=F32) + b_ref[...]
    o_ref[...] = h_ref[...] + _mod_slice(m_ref[...], 2) * out


def _conv_layer(h, mod, cond_of_tile, seq_first, seq_last, g, w_in, b_in, dw, dw_b, ng, w_out, b_out):
    n_tok, d = h.shape
    nt = n_tok // TM
    hb = TM // HALO
    glu = pl.pallas_call(
        _conv_in_kernel,
        grid_spec=pltpu.PrefetchScalarGridSpec(
            num_scalar_prefetch=1, grid=(nt,),
            in_specs=[
                pl.BlockSpec((TM, d), lambda i, cond: (i, 0)),
                pl.BlockSpec((None, 1, 6 * d), lambda i, cond: (cond[i], 0, 0)),
                pl.BlockSpec((1, d), lambda i, cond: (0, 0)),
                pl.BlockSpec((d, 2 * d), lambda i, cond: (0, 0)),
                pl.BlockSpec((1, 2 * d), lambda i, cond: (0, 0)),
            ],
            out_specs=pl.BlockSpec((TM, d), lambda i, cond: (i, 0))),
        out_shape=jax.ShapeDtypeStruct((n_tok, d), F32),
        compiler_params=_cparams(1),
        name="conv_in",
    )(cond_of_tile, h, mod, g.reshape(1, d), w_in, b_in.reshape(1, 2 * d))

    n_hb = n_tok // HALO
    dw_pad = jnp.zeros((32, d), F32).at[:CONV_K].set(dw)
    return pl.pallas_call(
        _conv_out_kernel,
        grid_spec=pltpu.PrefetchScalarGridSpec(
            num_scalar_prefetch=3, grid=(nt,),
            in_specs=[
                pl.BlockSpec((TM, d), lambda i, f, l, cond: (i, 0)),
                pl.BlockSpec((HALO, d), lambda i, f, l, cond: (jnp.maximum(i * hb - 1, 0), 0)),
                pl.BlockSpec((HALO, d), lambda i, f, l, cond: (jnp.minimum((i + 1) * hb, n_hb - 1), 0)),
                pl.BlockSpec((32, d), lambda i, f, l, cond: (0, 0)),
                pl.BlockSpec((1, d), lambda i, f, l, cond: (0, 0)),
                pl.BlockSpec((1, d), lambda i, f, l, cond: (0, 0)),
                pl.BlockSpec((d, d), lambda i, f, l, cond: (0, 0)),
                pl.BlockSpec((1, d), lambda i, f, l, cond: (0, 0)),
                pl.BlockSpec((TM, d), lambda i, f, l, cond: (i, 0)),
                pl.BlockSpec((None, 1, 6 * d), lambda i, f, l, cond: (cond[i], 0, 0)),
            ],
            out_specs=pl.BlockSpec((TM, d), lambda i, f, l, cond: (i, 0)),
            scratch_shapes=[pltpu.VMEM((TM + 2 * HALO, d), F32), pltpu.VMEM((TM, d), F32)]),
        out_shape=jax.ShapeDtypeStruct((n_tok, d), F32),
        compiler_params=_cparams(1),
        name="conv_out",
    )(seq_first, seq_last, cond_of_tile, glu, glu, glu, dw_pad, dw_b.reshape(1, d), ng.reshape(1, d),
      w_out, b_out.reshape(1, d), h, mod)


def _rope(y, cos, sin):
    lane = lax.broadcasted_iota(jnp.int32, y.shape, 1)
    nxt = pltpu.roll(y, LANES - 1, 1)
    prv = pltpu.roll(y, 1, 1)
    partner = jnp.where((lane & 1) == 0, nxt, prv)
    return y * cos + partner * sin


def _attn_latent_kernel(cond_ref, h_ref, m_ref, g_ref, qaw_ref, qan_ref, kvw_ref, kvn_ref, kpew_ref,
                        qlat_ref, ckv_ref, kpe_ref):
    del cond_ref
    m = m_ref[...]
    n = _modulate(h_ref[...], g_ref[...], _mod_slice(m, 0), _mod_slice(m, 1)).astype(BF16)
    qa = jnp.dot(n, qaw_ref[...], preferred_element_type=F32)
    qlat_ref[...] = _rms(qa, qan_ref[...]).astype(BF16)
    kv = jnp.dot(n, kvw_ref[...], preferred_element_type=F32)
    ckv_ref[...] = _rms(kv, kvn_ref[...])
    kpe_ref[...] = jnp.dot(n, kpew_ref[...], preferred_element_type=F32)


def _q_kernel(qlat_ref, w_ref, qn_ref, cos_ref, sin_ref, o_ref):
    q = jnp.dot(qlat_ref[...], w_ref[...], preferred_element_type=F32)
    cos = cos_ref[...]
    sin = sin_ref[...]
    qn = qn_ref[...]
    for hd in range(MLA_HEADS):
        cols = slice(hd * LANES, (hd + 1) * LANES)
        y = _rms(q[:, cols], qn, denom=QK_DIM)
        o_ref[:, cols] = _rope(y, cos, sin).astype(BF16)


def _kv_kernel(ckv_ref, kpe_ref, wk_ref, wv_ref, kn_ref, cos_ref, sin_ref, k_ref, v_ref):
    c = ckv_ref[...].astype(BF16)
    kf = jnp.dot(c, wk_ref[...], preferred_element_type=F32)
    v_ref[...] = jnp.dot(c, wv_ref[...], preferred_element_type=F32).astype(BF16)
    kpe = kpe_ref[...]
    cos = cos_ref[...]
    sin = sin_ref[...]
    kn = kn_ref[...]
    for hd in range(MLA_HEADS):
        cols = slice(hd * LANES, (hd + 1) * LANES)
        y = _rms(kf[:, cols] + kpe, kn, denom=QK_DIM)
        k_ref[:, cols] = _rope(y, cos, sin).astype(BF16)


def _attn_kernel(q_ref, k_ref, v_ref, o_ref):
    scale = float(QK_DIM) ** -0.5
    for hh in range(2):
        q = q_ref[:, hh * LANES:(hh + 1) * LANES]
        k = k_ref[:, hh * LANES:(hh + 1) * LANES]
        s = lax.dot_general(q, k, (((1,), (1,)), ((), ())), preferred_element_type=F32) * scale
        p = jnp.exp(s - jnp.max(s, axis=-1, keepdims=True))
        l = jnp.sum(p, axis=-1, keepdims=True)
        o = jnp.dot(p.astype(BF16), v_ref[:, hh * V_HEAD:(hh + 1) * V_HEAD], preferred_element_type=F32)
        o_ref[:, hh * V_HEAD:(hh + 1) * V_HEAD] = (o / l).astype(BF16)


def _attn_out_kernel(cond_ref, op_ref, os_ref, w_ref, h_ref, m_ref, o_ref, *, n_prompt_tiles):
    del cond_ref
    i = pl.program_id(0)
    o = jnp.where(i < n_prompt_tiles, op_ref[...], os_ref[...])
    out = jnp.dot(o, w_ref[...], preferred_element_type=F32)
    o_ref[...] = h_ref[...] + _mod_slice(m_ref[...], 2) * out


def _attention(q, k, v, n_batch, sq, sk, q_tile_offset):
    nq = sq // TM
    hp = MLA_HEADS // 2
    return pl.pallas_call(
        _attn_kernel,
        grid=(n_batch, hp, nq),
        in_specs=[
            pl.BlockSpec((TM, 2 * LANES), lambda b, h, qi: (q_tile_offset + b * nq + qi, h)),
            pl.BlockSpec((sk, 2 * LANES), lambda b, h, qi: (b, h)),
            pl.BlockSpec((sk, 2 * V_HEAD), lambda b, h, qi: (b, h)),
        ],
        out_specs=pl.BlockSpec((TM, 2 * V_HEAD), lambda b, h, qi: (b * nq + qi, h)),
        out_shape=jax.ShapeDtypeStruct((n_batch * sq, MLA_HEADS * V_HEAD), BF16),
        compiler_params=_cparams(3),
        name="attention",
    )(q, k, v)


def _kv_expand(ckv, kpe, wk, wv, kn, cos_t, sin_t, table_block):
    n_rows = ckv.shape[0]
    nt = n_rows // TM
    return pl.pallas_call(
        _kv_kernel,
        grid=(nt,),
        in_specs=[
            pl.BlockSpec((TM, ckv.shape[1]), lambda i: (i, 0)),
            pl.BlockSpec((TM, LANES), lambda i: (i, 0)),
            pl.BlockSpec(wk.shape, lambda i: (0, 0)),
            pl.BlockSpec(wv.shape, lambda i: (0, 0)),
            pl.BlockSpec((1, LANES), lambda i: (0, 0)),
            pl.BlockSpec((TM, LANES), lambda i: (table_block(i), 0)),
            pl.BlockSpec((TM, LANES), lambda i: (table_block(i), 0)),
        ],
        out_specs=[pl.BlockSpec((TM, MLA_HEADS * LANES), lambda i: (i, 0)),
                   pl.BlockSpec((TM, MLA_HEADS * V_HEAD), lambda i: (i, 0))],
        out_shape=[jax.ShapeDtypeStruct((n_rows, MLA_HEADS * LANES), BF16),
                   jax.ShapeDtypeStruct((n_rows, MLA_HEADS * V_HEAD), BF16)],
        compiler_params=_cparams(1),
        name="kv_expand",
    )(ckv, kpe, wk, wv, kn, cos_t, sin_t)


def _attn_layer(h, mod, cond_of_tile, g, w, cache_ckv, cache_kpe, cos_t, sin_t, dims):
    n_tok, d = h.shape
    nt = n_tok // TM
    n_prompt, seq, n_dec, dec_seq, past = dims
    npt = n_prompt * seq // TM
    tps = dec_seq // TM
    q_lora = w["q_a_w"].shape[1]
    kv_lora = w["kv_w"].shape[1]

    qlat, ckv, kpe = pl.pallas_call(
        _attn_latent_kernel,
        grid_spec=pltpu.PrefetchScalarGridSpec(
            num_scalar_prefetch=1, grid=(nt,),
            in_specs=[
                pl.BlockSpec((TM, d), lambda i, cond: (i, 0)),
                pl.BlockSpec((None, 1, 6 * d), lambda i, cond: (cond[i], 0, 0)),
                pl.BlockSpec((1, d), lambda i, cond: (0, 0)),
                pl.BlockSpec((d, q_lora), lambda i, cond: (0, 0)),
                pl.BlockSpec((1, q_lora), lambda i, cond: (0, 0)),
                pl.BlockSpec((d, kv_lora), lambda i, cond: (0, 0)),
                pl.BlockSpec((1, kv_lora), lambda i, cond: (0, 0)),
                pl.BlockSpec((d, LANES), lambda i, cond: (0, 0)),
            ],
            out_specs=[pl.BlockSpec((TM, q_lora), lambda i, cond: (i, 0)),
                       pl.BlockSpec((TM, kv_lora), lambda i, cond: (i, 0)),
                       pl.BlockSpec((TM, LANES), lambda i, cond: (i, 0))]),
        out_shape=[jax.ShapeDtypeStruct((n_tok, q_lora), BF16),
                   jax.ShapeDtypeStruct((n_tok, kv_lora), F32),
                   jax.ShapeDtypeStruct((n_tok, LANES), F32)],
        compiler_params=_cparams(1),
        name="attn_latent",
    )(cond_of_tile, h, mod, g.reshape(1, d), w["q_a_w"], w["q_a_norm"], w["kv_w"], w["kv_norm"], w["kpe_w"])

    q_table = lambda i: jnp.where(i < npt, 0, 1 + (i - npt) % tps)
    q = pl.pallas_call(
        _q_kernel,
        grid=(nt,),
        in_specs=[
            pl.BlockSpec((TM, q_lora), lambda i: (i, 0)),
            pl.BlockSpec(w["q_b_w"].shape, lambda i: (0, 0)),
            pl.BlockSpec((1, LANES), lambda i: (0, 0)),
            pl.BlockSpec((TM, LANES), lambda i: (q_table(i), 0)),
            pl.BlockSpec((TM, LANES), lambda i: (q_table(i), 0)),
        ],
        out_specs=pl.BlockSpec((TM, MLA_HEADS * LANES), lambda i: (i, 0)),
        out_shape=jax.ShapeDtypeStruct((n_tok, MLA_HEADS * LANES), BF16),
        compiler_params=_cparams(1),
        name="q_expand",
    )(qlat, w["q_b_w"], w["q_norm"], cos_t, sin_t)

    n_p_rows = n_prompt * seq
    ckv_p, kpe_p = ckv[:n_p_rows], kpe[:n_p_rows]
    k_p, v_p = _kv_expand(ckv_p, kpe_p, w["k_w"], w["v_w"], w["k_norm"], cos_t, sin_t, lambda i: 0)
    ckv_s = jnp.concatenate([cache_ckv, ckv[n_p_rows:].reshape(n_dec, dec_seq, kv_lora)], axis=1)
    kpe_s = jnp.concatenate([cache_kpe, kpe[n_p_rows:].reshape(n_dec, dec_seq, LANES)], axis=1)
    sk = past + dec_seq
    tpk = sk // TM
    pc = past // TM
    k_s, v_s = _kv_expand(ckv_s.reshape(n_dec * sk, kv_lora), kpe_s.reshape(n_dec * sk, LANES),
                          w["k_w"], w["v_w"], w["k_norm"], cos_t, sin_t,
                          lambda i: jnp.where(i % tpk < pc, 0, 1 + i % tpk - pc))

    o_p = _attention(q, k_p, v_p, n_prompt, seq, seq, 0)
    o_s = _attention(q, k_s, v_s, n_dec, dec_seq, sk, npt)

    h_new = pl.pallas_call(
        functools.partial(_attn_out_kernel, n_prompt_tiles=npt),
        grid_spec=pltpu.PrefetchScalarGridSpec(
            num_scalar_prefetch=1, grid=(nt,),
            in_specs=[
                pl.BlockSpec((TM, d), lambda i, cond: (jnp.minimum(i, npt - 1), 0)),
                pl.BlockSpec((TM, d), lambda i, cond: (jnp.maximum(i - npt, 0), 0)),
                pl.BlockSpec((d, d), lambda i, cond: (0, 0)),
                pl.BlockSpec((TM, d), lambda i, cond: (i, 0)),
                pl.BlockSpec((None, 1, 6 * d), lambda i, cond: (cond[i], 0, 0)),
            ],
            out_specs=pl.BlockSpec((TM, d), lambda i, cond: (i, 0))),
        out_shape=jax.ShapeDtypeStruct((n_tok, d), F32),
        compiler_params=_cparams(1),
        name="attn_out",
    )(cond_of_tile, o_p, o_s, w["o_w"], h, mod)
    return h_new, ckv_p, kpe_p[:, QK_NOPE:QK_NOPE + QK_ROPE]


def _topk_rows(s, k):
    n = s.shape[0]
    row = lax.broadcasted_iota(jnp.int32, s.shape, 0).astype(F32)
    rank = jnp.full(s.shape, float(k), F32)
    vals = []
    for r in range(k):
        m = jnp.max(s, axis=0, keepdims=True)
        pos = jnp.min(jnp.where(s == m, row, float(n)), axis=0, keepdims=True)
        hit = row == pos
        vals.append(m)
        s = jnp.where(hit, -jnp.inf, s)
        rank = jnp.where(hit, float(r), rank)
    return jnp.concatenate(vals, axis=0), rank


def _peer_route_kernel(cond_ref, h_ref, m_ref, g_ref, wqt_ref, keys_ref,
                       x_ref, r1_ref, cut_ref, e_ref,
                       qt_ref, sv_ref, r0_ref):
    del cond_ref
    m = m_ref[...]
    n = _modulate(h_ref[...], g_ref[...], _mod_slice(m, 3), _mod_slice(m, 4)).astype(BF16)
    x_ref[...] = n
    qt_ref[...] = lax.dot_general(wqt_ref[...], n, (((1,), (1,)), ((), ())), preferred_element_type=F32)

    def sub_keys(hd, carry):
        for p, rank_ref in enumerate((r0_ref, r1_ref)):
            hp = 2 * hd + p
            q = qt_ref[pl.ds(pl.multiple_of(hp * N_KEYS, N_KEYS), N_KEYS), :].astype(BF16)
            s = jnp.dot(keys_ref[p], q, preferred_element_type=F32)
            vals, rank = _topk_rows(s, PEER_TOPK)
            sv_ref[hp] = vals
            rank_ref[hd] = rank
            e_ref[hp] = jnp.exp(s - vals[0:1])
        return carry

    lax.fori_loop(0, PEER_HEADS, sub_keys, 0)

    def head(hd, carry):
        sv0, sv1 = sv_ref[2 * hd], sv_ref[2 * hd + 1]
        widths = [min(PEER_TOPK, (PEER_TOPK + 1) // (a + 1)) for a in range(PEER_TOPK)]
        pad = -sum(widths) % SUBLANES
        cand = jnp.concatenate([sv0[a:a + 1] + sv1[0:w] for a, w in enumerate(widths)]
                               + [jnp.full((pad, TR), -jnp.inf, F32)], axis=0)
        top_s, rank = _topk_rows(cand, PEER_TOPK)
        p = jnp.exp(top_s - top_s[0:1])
        e_ref[2 * hd + 1] = e_ref[2 * hd + 1] / jnp.sum(p, axis=0, keepdims=True)
        chosen = jnp.where(rank < float(PEER_TOPK), 1.0, 0.0)
        r0 = r0_ref[hd]
        cut = jnp.zeros((N_KEYS, TR), F32)
        start = 0
        for a, w in enumerate(widths):
            count = jnp.sum(chosen[start:start + w], axis=0, keepdims=True)
            cut = jnp.where(r0 == float(a), count, cut)
            start += w
        cut_ref[hd] = cut
        return carry

    lax.fori_loop(0, PEER_HEADS, head, 0)


def _peer_route(h, mod, cond_of_tile, g, wqt, keys):
    n_tok, d = h.shape
    nt = n_tok // TR
    nq = wqt.shape[0]
    return pl.pallas_call(
        _peer_route_kernel,
        grid_spec=pltpu.PrefetchScalarGridSpec(
            num_scalar_prefetch=1, grid=(nt,),
            in_specs=[
                pl.BlockSpec((TR, d), lambda i, cond: (i, 0)),
                pl.BlockSpec((None, 1, 6 * d), lambda i, cond: (cond[i * (TR // TM)], 0, 0)),
                pl.BlockSpec((1, d), lambda i, cond: (0, 0)),
                pl.BlockSpec((nq, d), lambda i, cond: (0, 0)),
                pl.BlockSpec(keys.shape, lambda i, cond: (0, 0, 0)),
            ],
            out_specs=[pl.BlockSpec((TR, d), lambda i, cond: (i, 0)),
                       pl.BlockSpec((PEER_HEADS, N_KEYS, TR), lambda i, cond: (0, 0, i)),
                       pl.BlockSpec((PEER_HEADS, N_KEYS, TR), lambda i, cond: (0, 0, i)),
                       pl.BlockSpec((2 * PEER_HEADS, N_KEYS, TR), lambda i, cond: (0, 0, i))],
            scratch_shapes=[
                pltpu.VMEM((nq, TR), F32),
                pltpu.VMEM((2 * PEER_HEADS, PEER_TOPK, TR), F32),
                pltpu.VMEM((PEER_HEADS, N_KEYS, TR), F32),
            ]),
        out_shape=[jax.ShapeDtypeStruct((n_tok, d), BF16),
                   jax.ShapeDtypeStruct((PEER_HEADS, N_KEYS, n_tok), F32),
                   jax.ShapeDtypeStruct((PEER_HEADS, N_KEYS, n_tok), F32),
                   jax.ShapeDtypeStruct((2 * PEER_HEADS, N_KEYS, n_tok), F32)],
        compiler_params=_cparams(1),
        name="peer_route",
    )(cond_of_tile, h, mod, g.reshape(1, d), wqt, keys)


def _peer_dense_kernel(cond_ref, x_ref, r1_ref, cut_ref, e_ref, u_ref, vt_ref, h_ref, m_ref, o_ref,
                       st_ref, a_ref, acc_ref):
    del cond_ref
    j = pl.program_id(1)
    n_a = EB // N_KEYS
    assert n_a == SUBLANES
    a_rows = pl.ds(pl.multiple_of(j * n_a, n_a), n_a)

    @pl.when(j == 0)
    def _():
        acc_ref[...] = jnp.zeros_like(acc_ref)

    st_ref[...] = lax.dot_general(u_ref[...], x_ref[...], (((1,), (1,)), ((), ())), preferred_element_type=F32)
    for lc in range(TD // LANES):
        lanes = slice(lc * LANES, (lc + 1) * LANES)
        for al in range(n_a):
            rows = slice(al * N_KEYS, (al + 1) * N_KEYS)
            g = jnp.zeros((N_KEYS, LANES), F32)
            for hd in range(PEER_HEADS):
                cut = cut_ref[hd, a_rows, lanes][al:al + 1]
                e0 = e_ref[2 * hd, a_rows, lanes][al:al + 1]
                wgt = e0 * e_ref[2 * hd + 1, :, lanes]
                g = g + jnp.where(r1_ref[hd, :, lanes] < cut, wgt, 0.0)
            a_ref[rows, lanes] = (g * _gelu(st_ref[rows, lanes])).astype(BF16)
    acc_ref[...] += jnp.dot(vt_ref[...], a_ref[...], preferred_element_type=F32)

    @pl.when(j == pl.num_programs(1) - 1)
    def _():
        o_ref[...] = h_ref[...] + _mod_slice(m_ref[...], 5) * acc_ref[...].T


def _peer_dense(h, x, r1, cut, e, mod, cond_of_tile, u_bf, vt_bf, layer):
    n_tok, d = h.shape
    n_blocks = u_bf.shape[1] // EB
    return pl.pallas_call(
        _peer_dense_kernel,
        grid_spec=pltpu.PrefetchScalarGridSpec(
            num_scalar_prefetch=1, grid=(n_tok // TD, n_blocks),
            in_specs=[
                pl.BlockSpec((TD, d), lambda i, j, cond: (i, 0)),
                pl.BlockSpec((PEER_HEADS, N_KEYS, TD), lambda i, j, cond: (0, 0, i)),
                pl.BlockSpec((PEER_HEADS, N_KEYS, TD), lambda i, j, cond: (0, 0, i)),
                pl.BlockSpec((2 * PEER_HEADS, N_KEYS, TD), lambda i, j, cond: (0, 0, i)),
                pl.BlockSpec((None, EB, d), lambda i, j, cond: (layer, j, 0)),
                pl.BlockSpec((None, d, EB), lambda i, j, cond: (layer, 0, j)),
                pl.BlockSpec((TD, d), lambda i, j, cond: (i, 0)),
                pl.BlockSpec((None, 1, 6 * d), lambda i, j, cond: (cond[i * (TD // TM)], 0, 0)),
            ],
            out_specs=pl.BlockSpec((TD, d), lambda i, j, cond: (i, 0)),
            scratch_shapes=[
                pltpu.VMEM((EB, TD), F32),
                pltpu.VMEM((EB, TD), BF16),
                pltpu.VMEM((d, TD), F32),
            ]),
        out_shape=jax.ShapeDtypeStruct((n_tok, d), F32),
        compiler_params=pltpu.CompilerParams(
            dimension_semantics=("arbitrary", "arbitrary"), vmem_limit_bytes=DENSE_VMEM_LIMIT),
        name="peer_dense",
    )(cond_of_tile, x, r1, cut, e, u_bf, vt_bf, h, mod)


def _rope_tables(seq_len):
    rows = seq_len // GRID_W
    row = jnp.repeat(jnp.arange(rows), GRID_W).astype(F32)
    col = jnp.tile(jnp.arange(GRID_W), rows).astype(F32)
    half = QK_ROPE // 2
    inv = ROPE_THETA ** (-(jnp.arange(half // 2, dtype=F32) * 2.0 / half))
    ang = jnp.concatenate([row[:, None] * inv, col[:, None] * inv], axis=-1)
    cos, sin = jnp.cos(ang), jnp.sin(ang)
    cos_pairs = jnp.repeat(cos, 2, axis=-1)
    sin_pairs = jnp.stack([-sin, sin], axis=-1).reshape(seq_len, QK_ROPE)
    cos_t = jnp.ones((TM + seq_len, LANES), F32).at[TM:, QK_NOPE:QK_DIM].set(cos_pairs)
    sin_t = jnp.zeros((TM + seq_len, LANES), F32).at[TM:, QK_NOPE:QK_DIM].set(sin_pairs)
    return cos_t, sin_t


def _pad_heads(w, width):
    k = w.shape[0]
    w3 = w.reshape(k, MLA_HEADS, width)
    return jnp.pad(w3, ((0, 0), (0, 0), (0, LANES - width))).reshape(k, MLA_HEADS * LANES)


def _attn_weights(j, q_a_w, q_a_norm, q_b_w, kv_a_w, kv_a_norm, kv_b_w, q_norm, k_norm, o_w):
    kv_lora = kv_a_norm.shape[1]
    kvb = kv_b_w[j].reshape(kv_lora, MLA_HEADS, QK_NOPE + V_HEAD)
    lane_pad = (0, LANES - QK_DIM)
    return {
        "q_a_w": q_a_w[j].astype(BF16),
        "q_a_norm": q_a_norm[j].reshape(1, -1),
        "q_b_w": _pad_heads(q_b_w[j], QK_DIM).astype(BF16),
        "kv_w": kv_a_w[j][:, :kv_lora].astype(BF16),
        "kv_norm": kv_a_norm[j].reshape(1, -1),
        "kpe_w": jnp.pad(kv_a_w[j][:, kv_lora:], ((0, 0), (QK_NOPE, LANES - QK_DIM))).astype(BF16),
        "k_w": _pad_heads(kvb[:, :, :QK_NOPE].reshape(kv_lora, MLA_HEADS * QK_NOPE), QK_NOPE).astype(BF16),
        "v_w": kvb[:, :, QK_NOPE:].reshape(kv_lora, MLA_HEADS * V_HEAD).astype(BF16),
        "q_norm": jnp.pad(q_norm[j], lane_pad).reshape(1, LANES),
        "k_norm": jnp.pad(k_norm[j], lane_pad).reshape(1, LANES),
        "o_w": o_w[j].astype(BF16),
    }


def kernel(x_prompt, x_sample, cache_ckv, cache_kpe, c, c_ctx, mod_w, mod_b, norm1_g, norm2_g, conv_w_in, conv_b_in, conv_dw, conv_dw_b, conv_norm_g, conv_w_out, conv_b_out, q_a_w, q_a_norm, q_b_w, kv_a_w, kv_a_norm, kv_b_w, q_norm, k_norm, o_w, peer_wq, peer_keys, peer_u, peer_v):
    n_prompt, seq, d = x_prompt.shape
    n_dec, dec_seq, _ = x_sample.shape
    past = cache_ckv.shape[2]
    depth = mod_w.shape[0]
    assert d == D_MODEL and seq % TM == 0 and dec_seq % TM == 0 and past % TM == 0
    assert 1 + n_dec <= 8
    assert (n_prompt * seq) % TD == 0 and dec_seq % TD == 0 and peer_u.shape[1] % EB == 0
    dims = (n_prompt, seq, n_dec, dec_seq, past)

    n_p_rows = n_prompt * seq
    h = jnp.concatenate([x_prompt.reshape(n_p_rows, d), x_sample.reshape(n_dec * dec_seq, d)], axis=0)
    n_tok = h.shape[0]

    tiles_p, tps = n_p_rows // TM, dec_seq // TM
    tile = jnp.arange(n_tok // TM, dtype=jnp.int32)
    in_p = tile < tiles_p
    cond_of_tile = jnp.where(in_p, 0, 1 + (tile - tiles_p) // tps).astype(jnp.int32)
    pos_p, pos_s = tile % (seq // TM), (tile - tiles_p) % tps
    seq_first = jnp.where(in_p, pos_p == 0, pos_s == 0).astype(jnp.int32)
    seq_last = jnp.where(in_p, pos_p == seq // TM - 1, pos_s == tps - 1).astype(jnp.int32)

    cond8 = jnp.zeros((8, d), F32).at[0].set(c_ctx).at[1:1 + n_dec].set(c)
    mod_all = _modulation_all(cond8, mod_w, mod_b)
    u_bf = peer_u.astype(BF16)
    vt_bf = jnp.swapaxes(peer_v.astype(BF16), 1, 2)
    cos_t, sin_t = _rope_tables(dec_seq)
    cache_kpe_pad = jnp.pad(cache_kpe, ((0, 0), (0, 0), (0, 0), (QK_NOPE, LANES - QK_DIM)))

    new_ckv, new_kpe = [], []
    for i in range(depth):
        mod = mod_all[i].reshape(8, 1, 6 * d)
        j = i // 2
        if i % 2 == 0:
            h = _conv_layer(h, mod, cond_of_tile, seq_first, seq_last, norm1_g[i],
                            conv_w_in[j].astype(BF16), conv_b_in[j], conv_dw[j], conv_dw_b[j],
                            conv_norm_g[j], conv_w_out[j].astype(BF16), conv_b_out[j])
        else:
            w = _attn_weights(j, q_a_w, q_a_norm, q_b_w, kv_a_w, kv_a_norm, kv_b_w, q_norm, k_norm, o_w)
            h, ckv_p, kpe_p = _attn_layer(h, mod, cond_of_tile, norm1_g[i], w, cache_ckv[:, j],
                                          cache_kpe_pad[:, j], cos_t, sin_t, dims)
            new_ckv.append(ckv_p.reshape(n_prompt, seq, -1))
            new_kpe.append(kpe_p.reshape(n_prompt, seq, -1))
        x, r1, cut, e = _peer_route(h, mod, cond_of_tile, norm2_g[i], peer_wq[i].T.astype(BF16),
                                    peer_keys[i].astype(BF16))
        h = _peer_dense(h, x, r1, cut, e, mod, cond_of_tile, u_bf, vt_bf, i)

    y_prompt = h[:n_p_rows].reshape(n_prompt, seq, d)
    y_sample = h[n_p_rows:].reshape(n_dec, dec_seq, d)
    return (y_prompt, y_sample, jnp.stack(new_ckv, axis=1), jnp.stack(new_kpe, axis=1))
```
